```python
import jax, jax.numpy as jnp
from jax import lax
import numpy as np

D_MODEL = 1024
BATCH = 8
SEQ = 4096
DEPTH = 2

HEAD_DIM = 64
BLOCK = 128
SWA_Q_HEADS = 8
SWA_KV_HEADS = 2
SWA_GROUP = SWA_Q_HEADS // SWA_KV_HEADS
SWA_WINDOW = 128
SB_HEADS = 8
ROPE_THETA = 10000.0
SWA_WIDTH = SWA_Q_HEADS * HEAD_DIM
SWA_KV_WIDTH = SWA_KV_HEADS * HEAD_DIM
SB_WIDTH = SB_HEADS * HEAD_DIM
N_BRANCHES = 2
OFF_K_SWA = SWA_WIDTH
OFF_V_SWA = OFF_K_SWA + SWA_KV_WIDTH
OFF_Q_SB = OFF_V_SWA + SWA_KV_WIDTH
OFF_K_SB = OFF_Q_SB + SB_WIDTH
OFF_V_SB = OFF_K_SB + SB_WIDTH
OFF_GATE_SWA = OFF_V_SB + SB_WIDTH
OFF_GATE_SB = OFF_GATE_SWA + D_MODEL
IN_COLS = OFF_GATE_SB + D_MODEL
IN_SPLITS = (OFF_K_SWA, OFF_V_SWA, OFF_Q_SB, OFF_K_SB, OFF_V_SB, OFF_GATE_SWA, OFF_GATE_SB)
PEER_HEADS = 8
PEER_NKEYS = 128
PEER_EXPERTS = PEER_NKEYS * PEER_NKEYS
PEER_TOPK = 16
PEER_QDIM = 256
PEER_HALF = PEER_QDIM // 2
PEER_CHUNK = 128
N_MOD = 6
NORM_EPS = 1e-6
NEG_INF = -1e30

kernel_name = "hybrid_swa_stickbreak_peer_adaln"


def rms_norm(x, g):
    x32 = x.astype(jnp.float32)
    y = x32 * lax.rsqrt(jnp.mean(x32 * x32, axis=-1, keepdims=True) + NORM_EPS)
    return (y * g.astype(jnp.float32)).astype(x.dtype)


def apply_rope(x, positions):
    dh = x.shape[-1]
    half = dh // 2
    inv_freq = jnp.power(ROPE_THETA, -jnp.arange(half, dtype=jnp.float32) * (2.0 / dh))
    ang = positions.astype(jnp.float32)[..., None] * inv_freq
    cos = jnp.cos(ang)[:, :, None, :]
    sin = jnp.sin(ang)[:, :, None, :]
    x32 = x.astype(jnp.float32)
    x1, x2 = x32[..., :half], x32[..., half:]
    return jnp.concatenate([x1 * cos - x2 * sin, x2 * cos + x1 * sin], axis=-1).astype(x.dtype)


def sliding_window_attention(q, k, v, sinks):
    B, S = q.shape[0], q.shape[1]
    nb = S // BLOCK
    qb = q.reshape(B, nb, BLOCK, SWA_KV_HEADS, SWA_GROUP, HEAD_DIM)

    def banded(t):
        tb = t.reshape(B, nb, BLOCK, SWA_KV_HEADS, HEAD_DIM)
        prev = jnp.pad(tb, ((0, 0), (1, 0), (0, 0), (0, 0), (0, 0)))[:, :-1]
        return jnp.concatenate([prev, tb], axis=2)

    kw, vw = banded(k), banded(v)
    scores = jnp.einsum('bnqhgd,bnkhd->bnhgqk', qb, kw).astype(jnp.float32) * (HEAD_DIM ** -0.5)
    qi = jnp.arange(BLOCK)[:, None] + BLOCK
    kj = jnp.arange(2 * BLOCK)[None, :]
    diff = qi - kj
    in_window = (diff >= 0) & (diff < SWA_WINDOW)
    key_abs = jnp.arange(nb)[:, None] * BLOCK - BLOCK + kj
    valid = in_window[None] & (key_abs >= 0)[:, None, :]
    scores = jnp.where(valid[None, :, None, None], scores, NEG_INF)
    sink = jnp.broadcast_to(
        sinks.astype(jnp.float32).reshape(1, 1, SWA_KV_HEADS, SWA_GROUP, 1, 1),
        scores.shape[:-1] + (1,))
    probs = jax.nn.softmax(jnp.concatenate([scores, sink], axis=-1), axis=-1)[..., :-1]
    out = jnp.einsum('bnhgqk,bnkhd->bnqhgd', probs.astype(v.dtype), vw)
    return out.reshape(B, S, SWA_WIDTH)


def stick_breaking_attention(q, k, v):
    B, S = q.shape[0], q.shape[1]
    nb = S // BLOCK
    qb = q.reshape(B, nb, BLOCK, SB_HEADS, HEAD_DIM).transpose(1, 0, 3, 2, 4)
    kt = k.transpose(0, 2, 1, 3)
    vt = v.transpose(0, 2, 1, 3)
    kpos = jnp.arange(S)

    def one_block(args):
        q_blk, blk = args
        z = jnp.einsum('bhqd,bhkd->bhqk', q_blk, kt).astype(jnp.float32) * (HEAD_DIM ** -0.5)
        qpos = blk * BLOCK + jnp.arange(BLOCK)
        strict = kpos[None, :] < qpos[:, None]
        log_1m = jnp.where(strict, jax.nn.log_sigmoid(-z), 0.0)
        between = lax.cumsum(log_1m, axis=3, reverse=True) - log_1m
        attn = jnp.where(strict, jnp.exp(jax.nn.log_sigmoid(z) + between), 0.0)
        return jnp.einsum('bhqk,bhkd->bhqd', attn.astype(vt.dtype), vt)

    out = lax.map(one_block, (qb, jnp.arange(nb)))
    return out.transpose(1, 0, 3, 2, 4).reshape(B, S, SB_WIDTH)


def peer_ffn(h, w_q, sub_keys, u, v):
    B, S, D = h.shape
    q = (h @ w_q).reshape(B, S, PEER_HEADS, 2, PEER_HALF)
    sk = jnp.einsum('bshpd,hpnd->bshpn', q, sub_keys).astype(jnp.float32)
    top_s, top_i = lax.top_k(sk, PEER_TOPK)
    n_cand = PEER_TOPK * PEER_TOPK
    cand_s = (top_s[..., 0, :, None] + top_s[..., 1, None, :]).reshape(B, S, PEER_HEADS, n_cand)
    cand_i = (top_i[..., 0, :, None] * PEER_NKEYS + top_i[..., 1, None, :]).reshape(B, S, PEER_HEADS, n_cand)
    best_s, best_pos = lax.top_k(cand_s, PEER_TOPK)
    expert_idx = jnp.take_along_axis(cand_i, best_pos, axis=-1)
    gates = jax.nn.softmax(best_s, axis=-1)
    n_sel = PEER_HEADS * PEER_TOPK
    nc = (B * S) // PEER_CHUNK
    h_c = h.reshape(nc, PEER_CHUNK, D)
    idx_c = expert_idx.reshape(nc, PEER_CHUNK, n_sel)
    g_c = gates.astype(h.dtype).reshape(nc, PEER_CHUNK, n_sel)

    def chunk(args):
        hc, ic, gc = args
        act = jax.nn.gelu(jnp.einsum('tkd,td->tk', jnp.take(u, ic, axis=0), hc), approximate=False)
        return jnp.einsum('tk,tkd->td', gc * act, jnp.take(v, ic, axis=0))

    return lax.map(chunk, (h_c, idx_c, g_c)).reshape(B, S, D)


def setup_inputs(seed: int = 0) -> dict:
    key = jax.random.key(seed)
    ks = jax.random.split(key, 20)
    f32 = jnp.float32
    d = D_MODEL
    x = jax.random.normal(ks[0], (BATCH, SEQ, d), f32)
    c = jax.random.normal(ks[1], (BATCH, d), f32)
    offsets = jax.random.randint(ks[2], (BATCH, 1), 0, 512, dtype=jnp.int32)
    positions = (offsets + jnp.arange(SEQ, dtype=jnp.int32)[None, :]).astype(jnp.int32)
    ada_w = jax.random.normal(ks[3], (DEPTH, d, N_MOD * d), f32) * d ** -0.5
    ada_b = jax.random.normal(ks[4], (DEPTH, N_MOD * d), f32) * 0.02
    norm1_g = 1.0 + 0.02 * jax.random.normal(ks[5], (DEPTH, d), f32)
    w_in = jax.random.normal(ks[6], (DEPTH, d, IN_COLS), f32) * d ** -0.5
    swa_sinks = jax.random.normal(ks[7], (DEPTH, SWA_Q_HEADS), f32) * 0.5
    w_branch_swa = jax.random.normal(ks[8], (DEPTH, SWA_WIDTH, d), f32) * SWA_WIDTH ** -0.5
    w_branch_sb = jax.random.normal(ks[9], (DEPTH, SB_WIDTH, d), f32) * SB_WIDTH ** -0.5
    w_out = jax.random.normal(ks[10], (DEPTH, d, d), f32) * d ** -0.5
    norm2_g = 1.0 + 0.02 * jax.random.normal(ks[11], (DEPTH, d), f32)
    peer_wq = jax.random.normal(ks[12], (DEPTH, d, PEER_HEADS * PEER_QDIM), f32) * d ** -0.5
    peer_subkeys = jax.random.normal(ks[13], (DEPTH, PEER_HEADS, 2, PEER_NKEYS, PEER_HALF), f32) * PEER_HALF ** -0.5
    peer_u = jax.random.normal(ks[14], (DEPTH, PEER_EXPERTS, d), f32) * d ** -0.5
    peer_v = jax.random.normal(ks[15], (DEPTH, PEER_EXPERTS, d), f32) * PEER_HEADS ** -0.5
    final_g = 1.0 + 0.02 * jax.random.normal(ks[16], (d,), f32)
    return {"x": x, "c": c, "positions": positions, "ada_w": ada_w, "ada_b": ada_b,
            "norm1_g": norm1_g, "w_in": w_in, "swa_sinks": swa_sinks,
            "w_branch_swa": w_branch_swa, "w_branch_sb": w_branch_sb, "w_out": w_out,
            "norm2_g": norm2_g, "peer_wq": peer_wq, "peer_subkeys": peer_subkeys,
            "peer_u": peer_u, "peer_v": peer_v, "final_g": final_g}


def reference(x, c, positions, ada_w, ada_b, norm1_g, w_in, swa_sinks, w_branch_swa, w_branch_sb,
              w_out, norm2_g, peer_wq, peer_subkeys, peer_u, peer_v, final_g):
    B, S = x.shape[0], x.shape[1]
    c_act = jax.nn.silu(c)
    for l in range(DEPTH):
        mod = c_act @ ada_w[l] + ada_b[l]
        shift1, scale1, gate1, shift2, scale2, gate2 = [m[:, None, :] for m in jnp.split(mod, N_MOD, axis=-1)]
        h = rms_norm(x, norm1_g[l]) * (1.0 + scale1) + shift1
        proj = h @ w_in[l]
        q_a, k_a, v_a, q_b, k_b, v_b, g_a, g_b = jnp.split(proj, IN_SPLITS, axis=-1)
        q_a = apply_rope(q_a.reshape(B, S, SWA_Q_HEADS, HEAD_DIM), positions)
        k_a = apply_rope(k_a.reshape(B, S, SWA_KV_HEADS, HEAD_DIM), positions)
        y_a = sliding_window_attention(q_a, k_a, v_a.reshape(B, S, SWA_KV_HEADS, HEAD_DIM), swa_sinks[l])
        y_b = stick_breaking_attention(q_b.reshape(B, S, SB_HEADS, HEAD_DIM),
                                       k_b.reshape(B, S, SB_HEADS, HEAD_DIM),
                                       v_b.reshape(B, S, SB_HEADS, HEAD_DIM))
        merged = (jax.nn.sigmoid(g_a) * (y_a @ w_branch_swa[l])
                  + jax.nn.sigmoid(g_b) * (y_b @ w_branch_sb[l]))
        x = x + gate1 * (merged @ w_out[l])
        h2 = rms_norm(x, norm2_g[l]) * (1.0 + scale2) + shift2
        x = x + gate2 * peer_ffn(h2, peer_wq[l], peer_subkeys[l], peer_u[l], peer_v[l])
    return rms_norm(x, final_g)
```

```python
import functools

import jax
import jax.numpy as jnp
import numpy as np
from jax import lax
from jax.experimental import pallas as pl
from jax.experimental.pallas import tpu as pltpu

F32 = jnp.float32
BF16 = jnp.bfloat16
I32 = jnp.int32

HEAD_DIM = 64
BLOCK = 128
SWA_Q_HEADS = 8
SWA_KV_HEADS = 2
SB_HEADS = 8
ROPE_THETA = 10000.0
SWA_WIDTH = SWA_Q_HEADS * HEAD_DIM
SWA_KV_WIDTH = SWA_KV_HEADS * HEAD_DIM
SB_WIDTH = SB_HEADS * HEAD_DIM
PEER_HEADS = 8
PEER_NKEYS = 128
PEER_TOPK = 16
PEER_QDIM = 256
PEER_HALF = PEER_QDIM // 2
N_MOD = 6
NORM_EPS = 1e-6
NEG_INF = -1e30

LANES = 128
VMEM_LIMIT_BYTES = 56 * 1024 * 1024

COL_GATE_SWA = 0
COL_GATE_SB = 1024
COL_Q_SWA = 2048
COL_K_SWA = COL_Q_SWA + SWA_WIDTH
COL_V_SWA = COL_K_SWA + SWA_KV_WIDTH
COL_Q_SB = COL_V_SWA + SWA_KV_WIDTH
COL_K_SB = COL_Q_SB + SB_WIDTH
COL_V_SB = COL_K_SB + SB_WIDTH
IN_COLS = COL_V_SB + SB_WIDTH

SB_CHUNK = 2 * BLOCK
PEER_SUB = 8
PEER_TOK = 128


def _dot(a, b):
    return jnp.dot(a, b, preferred_element_type=F32)


def _dot_nt(a, b):
    return lax.dot_general(a, b, (((1,), (1,)), ((), ())), preferred_element_type=F32)


def _split_bf16(a):
    hi = a.astype(BF16)
    lo = (a - hi.astype(F32)).astype(BF16)
    return hi, lo


def _params(*sem):
    return pltpu.CompilerParams(dimension_semantics=sem, vmem_limit_bytes=VMEM_LIMIT_BYTES)


def _resident(shape, index_map):
    return pl.BlockSpec(shape, index_map, pipeline_mode=pl.Buffered(1))


def _ada_kernel(c_ref, w_ref, b_ref, o_ref):
    c = c_ref[...]
    a = c * jax.nn.sigmoid(c)
    a_hi, a_lo = _split_bf16(a)
    w_hi, w_lo = _split_bf16(w_ref[0])
    acc = _dot(a_hi, w_hi) + _dot(a_hi, w_lo) + _dot(a_lo, w_hi)
    o_ref[0] = acc + b_ref[0]


def _ada_mod(c, ada_w, ada_b):
    depth, d, n = ada_w.shape
    b = c.shape[0]
    tn = n // 4
    return pl.pallas_call(
        _ada_kernel,
        grid=(depth, n // tn),
        in_specs=[
            pl.BlockSpec((b, d), lambda l, j: (0, 0)),
            pl.BlockSpec((1, d, tn), lambda l, j: (l, 0, j)),
            pl.BlockSpec((1, 1, tn), lambda l, j: (l, 0, j)),
        ],
        out_specs=pl.BlockSpec((1, b, tn), lambda l, j: (l, 0, j)),
        out_shape=jax.ShapeDtypeStruct((depth, b, n), F32),
        compiler_params=_params("arbitrary", "arbitrary"),
        name="ada_mod",
    )(c, ada_w, ada_b.reshape(depth, 1, n))


def _modulated_norm(x, g, scale, shift):
    ms = jnp.mean(x * x, axis=-1, keepdims=True)
    y = x * lax.rsqrt(ms + NORM_EPS) * g
    return y * (1.0 + scale) + shift


def _norm_proj_kernel(x_ref, g_ref, scale_ref, shift_ref, w_ref, o_ref, *, col_chunk):
    h = _modulated_norm(x_ref[...], g_ref[...], scale_ref[0], shift_ref[0]).astype(BF16)
    n = o_ref.shape[1]
    for c0 in range(0, n, col_chunk):
        c1 = min(c0 + col_chunk, n)
        o_ref[:, c0:c1] = _dot(h, w_ref[:, c0:c1]).astype(o_ref.dtype)


def _norm_proj(x2, g, scale, shift, w_bf16, seq, tm=512):
    t, d = x2.shape
    n = w_bf16.shape[1]
    per_batch = lambda i: ((i * tm) // seq, 0, 0)
    return pl.pallas_call(
        functools.partial(_norm_proj_kernel, col_chunk=1024),
        grid=(t // tm,),
        in_specs=[
            pl.BlockSpec((tm, d), lambda i: (i, 0)),
            pl.BlockSpec((1, d), lambda i: (0, 0)),
            pl.BlockSpec((1, 1, d), per_batch),
            pl.BlockSpec((1, 1, d), per_batch),
            _resident((d, n), lambda i: (0, 0)),
        ],
        out_specs=pl.BlockSpec((tm, n), lambda i: (i, 0)),
        out_shape=jax.ShapeDtypeStruct((t, n), BF16),
        compiler_params=_params("arbitrary"),
        name="norm_proj",
    )(x2, g.reshape(1, d), scale, shift, w_bf16)


def _swa_kernel(sink_ref, q_ref, kc_ref, kp_ref, vc_ref, vp_ref, pc_ref, pp_ref, invf_ref, o_ref):
    n = pl.program_id(1)
    lane = lax.broadcasted_iota(I32, (BLOCK, LANES), 1)
    first_half = (lane % HEAD_DIM) < (HEAD_DIM // 2)
    low_head = lane < HEAD_DIM
    invf = invf_ref[...]

    def rope_tables(pos_ref):
        ang = pos_ref[0].astype(F32) * invf
        s = jnp.sin(ang)
        return jnp.cos(ang), jnp.where(first_half, -s, s)

    def rope(x, tables):
        c, s_signed = tables
        swapped = jnp.where(first_half, pltpu.roll(x, LANES - HEAD_DIM // 2, 1),
                            pltpu.roll(x, HEAD_DIM // 2, 1))
        return x * c + swapped * s_signed

    tab_c = rope_tables(pc_ref)
    tab_p = rope_tables(pp_ref)
    k = jnp.concatenate([rope(kp_ref[0].astype(F32), tab_p), rope(kc_ref[0].astype(F32), tab_c)], axis=0)
    v = jnp.concatenate([vp_ref[0], vc_ref[0]], axis=0).astype(F32)
    k_var = (k.astype(BF16), pltpu.roll(k, HEAD_DIM, 1).astype(BF16))
    v_var = (v.astype(BF16), pltpu.roll(v, HEAD_DIM, 1).astype(BF16))

    row = lax.broadcasted_iota(I32, (BLOCK, 2 * BLOCK), 0)
    col = lax.broadcasted_iota(I32, (BLOCK, 2 * BLOCK), 1)
    diff = row + BLOCK - col
    valid = (diff >= 0) & (diff < BLOCK) & ((col >= BLOCK) | (n > 0))

    group = SWA_Q_HEADS // SWA_KV_HEADS
    for jt in range(SWA_Q_HEADS // 2):
        q_t = rope(q_ref[0, :, jt * LANES:(jt + 1) * LANES].astype(F32), tab_c) * (HEAD_DIM ** -0.5)
        outs = []
        for hh in range(2):
            j = 2 * jt + hh
            g = j // group
            qm = jnp.where(low_head == (hh == 0), q_t, 0.0).astype(BF16)
            s = _dot_nt(qm, k_var[0 if g == hh else 1])
            s = jnp.where(valid, s, NEG_INF)
            sink = sink_ref[j]
            m = jnp.maximum(jnp.max(s, axis=-1, keepdims=True), sink)
            p = jnp.exp(s - m)
            den = jnp.sum(p, axis=-1, keepdims=True) + jnp.exp(sink - m)
            outs.append(_dot(p.astype(BF16), v_var[0 if g == hh else 1]) / den)
        o_ref[0, :, jt * LANES:(jt + 1) * LANES] = jnp.where(low_head, outs[0], outs[1]).astype(o_ref.dtype)


def _swa(proj3, pos3, invf, sinks):
    b, s, _ = proj3.shape
    nb = s // BLOCK
    cur = lambda c: (lambda i, n: (i, n, c))
    prev = lambda c: (lambda i, n: (i, jnp.maximum(n - 1, 0), c))
    return pl.pallas_call(
        _swa_kernel,
        grid=(b, nb),
        in_specs=[
            pl.BlockSpec(memory_space=pltpu.SMEM),
            pl.BlockSpec((1, BLOCK, SWA_WIDTH), cur(COL_Q_SWA // SWA_WIDTH)),
            pl.BlockSpec((1, BLOCK, LANES), cur(COL_K_SWA // LANES)),
            pl.BlockSpec((1, BLOCK, LANES), prev(COL_K_SWA // LANES)),
            pl.BlockSpec((1, BLOCK, LANES), cur(COL_V_SWA // LANES)),
            pl.BlockSpec((1, BLOCK, LANES), prev(COL_V_SWA // LANES)),
            pl.BlockSpec((1, BLOCK, 1), cur(0)),
            pl.BlockSpec((1, BLOCK, 1), prev(0)),
            pl.BlockSpec((1, LANES), lambda i, n: (0, 0)),
        ],
        out_specs=pl.BlockSpec((1, BLOCK, SWA_WIDTH), lambda i, n: (i, n, 0)),
        out_shape=jax.ShapeDtypeStruct((b, s, SWA_WIDTH), BF16),
        compiler_params=_params("arbitrary", "arbitrary"),
        name="swa",
    )(sinks, proj3, proj3, proj3, proj3, proj3, pos3, pos3, invf)


def _sb_kernel(q_ref, k_ref, v_ref, tri_ref, o_ref):
    qb = pl.program_id(2)
    lane = lax.broadcasted_iota(I32, (BLOCK, LANES), 1)
    low_head = lane < HEAD_DIM
    q = q_ref[0].astype(F32) * (HEAD_DIM ** -0.5)
    row = lax.broadcasted_iota(I32, (BLOCK, SB_CHUNK), 0) + qb * BLOCK
    col = lax.broadcasted_iota(I32, (BLOCK, SB_CHUNK), 1)
    tri = tri_ref[...]
    n_chunks = qb // (SB_CHUNK // BLOCK) + 1

    outs = []
    for h in range(2):
        qm = jnp.where(low_head == (h == 0), q, 0.0).astype(BF16)

        def body(i, carry, qm=qm):
            acc, later = carry
            off = pl.multiple_of((n_chunks - 1 - i) * SB_CHUNK, SB_CHUNK)
            kc = k_ref[0, pl.ds(off, SB_CHUNK), :]
            vc = v_ref[0, pl.ds(off, SB_CHUNK), :]
            z = _dot_nt(qm, kc)
            strict = (col + off) < row
            softplus = jnp.maximum(z, 0.0) + jnp.log1p(jnp.exp(-jnp.abs(z)))
            log_1m = jnp.where(strict, -softplus, 0.0)
            l_hi, l_lo = _split_bf16(log_1m)
            incl = _dot(l_hi, tri) + _dot(l_lo, tri)
            attn = jnp.where(strict, jnp.exp(z + incl + later), 0.0)
            acc = acc + _dot(attn.astype(BF16), vc)
            later = later + jnp.sum(log_1m, axis=-1, keepdims=True)
            return acc, later

        acc, _ = lax.fori_loop(0, n_chunks, body,
                               (jnp.zeros((BLOCK, LANES), F32), jnp.zeros((BLOCK, 1), F32)))
        outs.append(acc)
    o_ref[0] = jnp.where(low_head, outs[0], outs[1]).astype(o_ref.dtype)


def _stick_breaking(proj3):
    b, s, _ = proj3.shape
    nb = s // BLOCK
    pairs = SB_HEADS // 2
    tri = (np.arange(SB_CHUNK)[:, None] >= np.arange(SB_CHUNK)[None, :])
    tri = jnp.asarray(tri, dtype=BF16)
    return pl.pallas_call(
        _sb_kernel,
        grid=(b, pairs, nb),
        in_specs=[
            pl.BlockSpec((1, BLOCK, LANES), lambda i, p, n: (i, n, COL_Q_SB // LANES + p)),
            pl.BlockSpec((1, s, LANES), lambda i, p, n: (i, 0, COL_K_SB // LANES + p)),
            pl.BlockSpec((1, s, LANES), lambda i, p, n: (i, 0, COL_V_SB // LANES + p)),
            pl.BlockSpec((SB_CHUNK, SB_CHUNK), lambda i, p, n: (0, 0)),
        ],
        out_specs=pl.BlockSpec((1, BLOCK, LANES), lambda i, p, n: (i, n, p)),
        out_shape=jax.ShapeDtypeStruct((b, s, SB_WIDTH), BF16),
        compiler_params=_params("arbitrary", "arbitrary", "arbitrary"),
        name="stick_breaking",
    )(proj3, proj3, proj3, tri)


def _merge_kernel(ya_ref, yb_ref, ga_ref, gb_ref, x_ref, gate_ref, wa_ref, wb_ref, wo_ref, o_ref):
    ma = _dot(ya_ref[...], wa_ref[...])
    mb = _dot(yb_ref[...], wb_ref[...])
    merged = (jax.nn.sigmoid(ga_ref[...].astype(F32)) * ma
              + jax.nn.sigmoid(gb_ref[...].astype(F32)) * mb)
    o_ref[...] = x_ref[...] + gate_ref[0] * _dot(merged.astype(BF16), wo_ref[...])


def _merge(ya, yb, proj, x2, gate, wa, wb, wo, seq, tm=512):
    t, d = x2.shape
    per_batch = lambda i: ((i * tm) // seq, 0, 0)
    return pl.pallas_call(
        _merge_kernel,
        grid=(t // tm,),
        in_specs=[
            pl.BlockSpec((tm, SWA_WIDTH), lambda i: (i, 0)),
            pl.BlockSpec((tm, SB_WIDTH), lambda i: (i, 0)),
            pl.BlockSpec((tm, d), lambda i: (i, COL_GATE_SWA // d)),
            pl.BlockSpec((tm, d), lambda i: (i, COL_GATE_SB // d)),
            pl.BlockSpec((tm, d), lambda i: (i, 0)),
            pl.BlockSpec((1, 1, d), per_batch),
            _resident((SWA_WIDTH, d), lambda i: (0, 0)),
            _resident((SB_WIDTH, d), lambda i: (0, 0)),
            _resident((d, d), lambda i: (0, 0)),
        ],
        out_specs=pl.BlockSpec((tm, d), lambda i: (i, 0)),
        out_shape=jax.ShapeDtypeStruct((t, d), F32),
        compiler_params=_params("arbitrary"),
        name="merge_out",
    )(ya, yb, proj, proj, x2, gate, wa, wb, wo)


def _extract_top(vals, payload, k):
    r = vals.shape[0]
    idx = lax.broadcasted_iota(I32, vals.shape, 0)
    top_v, top_p = [], []
    for _ in range(k):
        m = jnp.max(vals, axis=0, keepdims=True)
        first = jnp.min(jnp.where(vals == m, idx, r), axis=0, keepdims=True)
        hit = idx == first
        top_v.append(m)
        if payload is None:
            top_p.append(first)
        else:
            top_p.append(jnp.sum(jnp.where(hit, payload, 0), axis=0, keepdims=True))
        vals = jnp.where(hit, -jnp.inf, vals)
    return top_v, top_p


def _peer_route_kernel(x_ref, g_ref, scale_ref, shift_ref, wq_ref, sk_ref, h_ref, idx_ref, gate_ref):
    h = _modulated_norm(x_ref[...], g_ref[...], scale_ref[0], shift_ref[0])
    h_ref[...] = h
    q = _dot(h.astype(BF16), wq_ref[...]).astype(BF16)
    idx_rows, gate_rows = [], []
    for head in range(PEER_HEADS):
        halves = []
        for p in range(2):
            c0 = (head * 2 + p) * PEER_HALF
            scores = _dot_nt(sk_ref[head * 2 + p], q[:, c0:c0 + PEER_HALF])
            halves.append(_extract_top(scores, None, PEER_TOPK))
        (s0, i0), (s1, i1) = halves
        s1_all = jnp.concatenate(s1, axis=0)
        i1_all = jnp.concatenate(i1, axis=0)
        cand_s = jnp.concatenate([s0[i] + s1_all for i in range(PEER_TOPK)], axis=0)
        cand_i = jnp.concatenate([i0[i] * PEER_NKEYS + i1_all for i in range(PEER_TOPK)], axis=0)
        best_s, best_i = _extract_top(cand_s, cand_i, PEER_TOPK)
        e = [jnp.exp(s - best_s[0]) for s in best_s]
        den = e[0]
        for t in e[1:]:
            den = den + t
        idx_rows += best_i
        gate_rows += [t / den for t in e]
    idx_ref[...] = jnp.concatenate(idx_rows, axis=0)
    gate_ref[...] = jnp.concatenate(gate_rows, axis=0)


def _peer_route(x2, g, scale, shift, wq, sk, seq, tm=256):
    t, d = x2.shape
    nq = wq.shape[1]
    nsel = PEER_HEADS * PEER_TOPK
    per_batch = lambda i: ((i * tm) // seq, 0, 0)
    return pl.pallas_call(
        _peer_route_kernel,
        grid=(t // tm,),
        in_specs=[
            pl.BlockSpec((tm, d), lambda i: (i, 0)),
            pl.BlockSpec((1, d), lambda i: (0, 0)),
            pl.BlockSpec((1, 1, d), per_batch),
            pl.BlockSpec((1, 1, d), per_batch),
            _resident((d, nq), lambda i: (0, 0)),
            _resident(sk.shape, lambda i: (0, 0, 0)),
        ],
        out_specs=[
            pl.BlockSpec((tm, d), lambda i: (i, 0)),
            pl.BlockSpec((nsel, tm), lambda i: (0, i)),
            pl.BlockSpec((nsel, tm), lambda i: (0, i)),
        ],
        out_shape=[
            jax.ShapeDtypeStruct((t, d), F32),
            jax.ShapeDtypeStruct((nsel, t), I32),
            jax.ShapeDtypeStruct((nsel, t), F32),
        ],
        compiler_params=_params("arbitrary"),
        name="peer_route",
    )(x2, g.reshape(1, d), scale, shift, wq, sk)


def _peer_expert_kernel(idx_ref, gate_ref, h_ref, x_ref, mod_ref, tab_ref, o_ref, rows, sem):
    nsel = PEER_HEADS * PEER_TOPK
    d = h_ref.shape[1]
    n_sub = PEER_TOK // PEER_SUB
    batch_rows = PEER_SUB * nsel

    def start_batch(s, slot):
        for j in range(PEER_SUB):
            tok = s * PEER_SUB + j

            def start_row(kk, _, tok=tok, j=j):
                for u in range(8):
                    kq = kk * 8 + u
                    e = idx_ref[tok, kq]
                    pltpu.make_async_copy(tab_ref.at[pl.ds(e, 1)],
                                          rows.at[slot, pl.ds(j * nsel + kq, 1)],
                                          sem.at[slot]).start(priority=u % 2)
                return 0

            lax.fori_loop(0, nsel // 8, start_row, 0)

    def wait_batch(slot):
        pltpu.make_async_copy(tab_ref.at[pl.ds(0, batch_rows)], rows.at[slot], sem.at[slot]).wait()

    start_batch(0, 0)
    gate_t = gate_ref[...]

    def sub_tile(s, _):
        slot = s % 2

        @pl.when(s + 1 < n_sub)
        def _():
            start_batch(s + 1, 1 - slot)

        wait_batch(slot)
        for j in range(PEER_SUB):
            tok = s * PEER_SUB + j
            h_row = h_ref[pl.ds(tok, 1), :]
            u_rows = rows[slot, j * nsel:(j + 1) * nsel, 0:d]
            act = jnp.sum(u_rows * h_row, axis=-1, keepdims=True)
            gate_col = pltpu.roll(gate_t, (PEER_TOK - tok) & (PEER_TOK - 1), 1)[:, 0:1]
            w = gate_col * (0.5 * act * (1.0 + lax.erf(act * np.float32(np.sqrt(0.5)))))
            v_rows = rows[slot, j * nsel:(j + 1) * nsel, d:2 * d]
            out_row = jnp.sum(v_rows * w, axis=0, keepdims=True)
            o_ref[pl.ds(tok, 1), :] = x_ref[pl.ds(tok, 1), :] + mod_ref[0] * out_row
        return 0

    lax.fori_loop(0, n_sub, sub_tile, 0)


def _peer_experts(idx, gate_t, h2, x2, gate, table, seq):
    t, d = x2.shape
    nsel = PEER_HEADS * PEER_TOPK
    per_batch = lambda i: ((i * PEER_TOK) // seq, 0, 0)
    return pl.pallas_call(
        _peer_expert_kernel,
        grid=(t // PEER_TOK,),
        in_specs=[
            pl.BlockSpec((PEER_TOK, nsel), lambda i: (i, 0), memory_space=pltpu.SMEM),
            pl.BlockSpec((nsel, PEER_TOK), lambda i: (0, i)),
            pl.BlockSpec((PEER_TOK, d), lambda i: (i, 0)),
            pl.BlockSpec((PEER_TOK, d), lambda i: (i, 0)),
            pl.BlockSpec((1, 1, d), per_batch),
            pl.BlockSpec(memory_space=pl.ANY),
        ],
        out_specs=pl.BlockSpec((PEER_TOK, d), lambda i: (i, 0)),
        out_shape=jax.ShapeDtypeStruct((t, d), F32),
        scratch_shapes=[
            pltpu.VMEM((2, PEER_SUB * nsel, 2 * d), F32),
            pltpu.SemaphoreType.DMA((2,)),
        ],
        compiler_params=_params("arbitrary"),
        name="peer_experts",
    )(idx, gate_t, h2, x2, gate, table)


def _final_norm_kernel(x_ref, g_ref, o_ref):
    x = x_ref[...]
    ms = jnp.mean(x * x, axis=-1, keepdims=True)
    o_ref[...] = x * lax.rsqrt(ms + NORM_EPS) * g_ref[...]


def _final_norm(x2, g, tm=512):
    t, d = x2.shape
    return pl.pallas_call(
        _final_norm_kernel,
        grid=(t // tm,),
        in_specs=[pl.BlockSpec((tm, d), lambda i: (i, 0)), pl.BlockSpec((1, d), lambda i: (0, 0))],
        out_specs=pl.BlockSpec((tm, d), lambda i: (i, 0)),
        out_shape=jax.ShapeDtypeStruct((t, d), F32),
        compiler_params=_params("arbitrary"),
        name="final_norm",
    )(x2, g.reshape(1, d))


def _reorder_in_cols(w):
    n_qkv = SWA_WIDTH + 2 * SWA_KV_WIDTH + 3 * SB_WIDTH
    return jnp.concatenate([w[:, n_qkv:], w[:, :n_qkv]], axis=1)


def kernel(x, c, positions, ada_w, ada_b, norm1_g, w_in, swa_sinks, w_branch_swa, w_branch_sb,
           w_out, norm2_g, peer_wq, peer_subkeys, peer_u, peer_v, final_g):
    b, s, d = x.shape
    depth = ada_w.shape[0]
    t = b * s
    mod = _ada_mod(c, ada_w, ada_b)
    pos3 = positions.reshape(b, s, 1)
    inv_freq = jnp.power(ROPE_THETA, -jnp.arange(HEAD_DIM // 2, dtype=F32) * (2.0 / HEAD_DIM))
    invf = jnp.tile(inv_freq, LANES // (HEAD_DIM // 2)).reshape(1, LANES)
    x2 = x.reshape(t, d)
    for l in range(depth):
        m = mod[l].reshape(b, N_MOD, 1, d)
        shift1, scale1, gate1, shift2, scale2, gate2 = [m[:, i] for i in range(N_MOD)]
        w_in_l = _reorder_in_cols(w_in[l]).astype(BF16)
        proj = _norm_proj(x2, norm1_g[l], scale1, shift1, w_in_l, s)
        proj3 = proj.reshape(b, s, IN_COLS)
        y_a = _swa(proj3, pos3, invf, swa_sinks[l])
        y_b = _stick_breaking(proj3)
        x2 = _merge(y_a.reshape(t, SWA_WIDTH), y_b.reshape(t, SB_WIDTH), proj, x2, gate1,
                    w_branch_swa[l].astype(BF16), w_branch_sb[l].astype(BF16),
                    w_out[l].astype(BF16), s)
        sk = peer_subkeys[l].reshape(PEER_HEADS * 2, PEER_NKEYS, PEER_HALF).astype(BF16)
        h2, idx_t, gate_t = _peer_route(x2, norm2_g[l], scale2, shift2, peer_wq[l].astype(BF16), sk, s)
        table = jnp.concatenate([peer_u[l], peer_v[l]], axis=1)
        x2 = _peer_experts(idx_t.T, gate_t, h2, x2, gate2, table, s)
    return _final_norm(x2, final_g).reshape(b, s, d)
```

```python
import functools

import jax
import jax.numpy as jnp
import numpy as np
from jax import lax
from jax.experimental import pallas as pl
from jax.experimental.pallas import tpu as pltpu

F32 = jnp.float32
BF16 = jnp.bfloat16
I32 = jnp.int32

HEAD_DIM = 64
BLOCK = 128
SWA_Q_HEADS = 8
SWA_KV_HEADS = 2
SB_HEADS = 8
ROPE_THETA = 10000.0
SWA_WIDTH = SWA_Q_HEADS * HEAD_DIM
SWA_KV_WIDTH = SWA_KV_HEADS * HEAD_DIM
SB_WIDTH = SB_HEADS * HEAD_DIM
PEER_HEADS = 8
PEER_NKEYS = 128
PEER_TOPK = 16
PEER_QDIM = 256
PEER_HALF = PEER_QDIM // 2
N_MOD = 6
NORM_EPS = 1e-6
NEG_INF = -1e30

LANES = 128
SUBLANES = 8
VMEM_LIMIT_BYTES = 56 * 1024 * 1024

COL_GATE_SWA = 0
COL_GATE_SB = 1024
COL_Q_SWA = 2048
COL_K_SWA = COL_Q_SWA + SWA_WIDTH
COL_V_SWA = COL_K_SWA + SWA_KV_WIDTH
COL_Q_SB = COL_V_SWA + SWA_KV_WIDTH
COL_K_SB = COL_Q_SB + SB_WIDTH
COL_V_SB = COL_K_SB + SB_WIDTH
IN_COLS = COL_V_SB + SB_WIDTH

SB_CHUNK = 2 * BLOCK
PEER_SUB = 4
PEER_TOK = 128


def _dot(a, b):
    return jnp.dot(a, b, preferred_element_type=F32)


def _dot_nt(a, b):
    return lax.dot_general(a, b, (((1,), (1,)), ((), ())), preferred_element_type=F32)


def _split_bf16(a):
    hi = a.astype(BF16)
    lo = (a - hi.astype(F32)).astype(BF16)
    return hi, lo


def _params(*sem):
    return pltpu.CompilerParams(dimension_semantics=sem, vmem_limit_bytes=VMEM_LIMIT_BYTES)


def _resident(shape, index_map):
    return pl.BlockSpec(shape, index_map, pipeline_mode=pl.Buffered(1))


def _ada_kernel(c_ref, w_ref, b_ref, o_ref):
    c = c_ref[...]
    a = c * jax.nn.sigmoid(c)
    a_hi, a_lo = _split_bf16(a)
    w_hi, w_lo = _split_bf16(w_ref[0])
    acc = _dot(a_hi, w_hi) + _dot(a_hi, w_lo) + _dot(a_lo, w_hi)
    o_ref[0] = acc + b_ref[0]


def _ada_mod(c, ada_w, ada_b):
    depth, d, n = ada_w.shape
    b = c.shape[0]
    tn = n // 4
    return pl.pallas_call(
        _ada_kernel,
        grid=(depth, n // tn),
        in_specs=[
            pl.BlockSpec((b, d), lambda l, j: (0, 0)),
            pl.BlockSpec((1, d, tn), lambda l, j: (l, 0, j)),
            pl.BlockSpec((1, 1, tn), lambda l, j: (l, 0, j)),
        ],
        out_specs=pl.BlockSpec((1, b, tn), lambda l, j: (l, 0, j)),
        out_shape=jax.ShapeDtypeStruct((depth, b, n), F32),
        compiler_params=_params("arbitrary", "arbitrary"),
        name="ada_mod",
    )(c, ada_w, ada_b.reshape(depth, 1, n))


def _modulated_norm(x, g, scale, shift):
    ms = jnp.mean(x * x, axis=-1, keepdims=True)
    y = x * lax.rsqrt(ms + NORM_EPS) * g
    return y * (1.0 + scale) + shift


def _norm_proj_kernel(x_ref, g_ref, scale_ref, shift_ref, w_ref, o_ref, *, col_chunk):
    h = _modulated_norm(x_ref[...], g_ref[...], scale_ref[0], shift_ref[0]).astype(BF16)
    n = o_ref.shape[1]
    for c0 in range(0, n, col_chunk):
        c1 = min(c0 + col_chunk, n)
        o_ref[:, c0:c1] = _dot(h, w_ref[:, c0:c1]).astype(o_ref.dtype)


def _norm_proj(x2, g, scale, shift, w_bf16, seq, tm=512):
    t, d = x2.shape
    n = w_bf16.shape[1]
    per_batch = lambda i: ((i * tm) // seq, 0, 0)
    return pl.pallas_call(
        functools.partial(_norm_proj_kernel, col_chunk=1024),
        grid=(t // tm,),
        in_specs=[
            pl.BlockSpec((tm, d), lambda i: (i, 0)),
            pl.BlockSpec((1, d), lambda i: (0, 0)),
            pl.BlockSpec((1, 1, d), per_batch),
            pl.BlockSpec((1, 1, d), per_batch),
            _resident((d, n), lambda i: (0, 0)),
        ],
        out_specs=pl.BlockSpec((tm, n), lambda i: (i, 0)),
        out_shape=jax.ShapeDtypeStruct((t, n), BF16),
        compiler_params=_params("arbitrary"),
        name="norm_proj",
    )(x2, g.reshape(1, d), scale, shift, w_bf16)


def _swa_kernel(sink_ref, q_ref, kc_ref, kp_ref, vc_ref, vp_ref, pc_ref, pp_ref, invf_ref, o_ref):
    n = pl.program_id(1)
    lane = lax.broadcasted_iota(I32, (BLOCK, LANES), 1)
    first_half = (lane % HEAD_DIM) < (HEAD_DIM // 2)
    low_head = lane < HEAD_DIM
    invf = invf_ref[...]

    def rope_tables(pos_ref):
        ang = pos_ref[0].astype(F32) * invf
        s = jnp.sin(ang)
        return jnp.cos(ang), jnp.where(first_half, -s, s)

    def rope(x, tables):
        c, s_signed = tables
        swapped = jnp.where(first_half, pltpu.roll(x, LANES - HEAD_DIM // 2, 1),
                            pltpu.roll(x, HEAD_DIM // 2, 1))
        return x * c + swapped * s_signed

    tab_c = rope_tables(pc_ref)
    tab_p = rope_tables(pp_ref)
    k = jnp.concatenate([rope(kp_ref[0].astype(F32), tab_p), rope(kc_ref[0].astype(F32), tab_c)], axis=0)
    v = jnp.concatenate([vp_ref[0], vc_ref[0]], axis=0).astype(F32)
    k_var = (k.astype(BF16), pltpu.roll(k, HEAD_DIM, 1).astype(BF16))
    v_var = (v.astype(BF16), pltpu.roll(v, HEAD_DIM, 1).astype(BF16))

    row = lax.broadcasted_iota(I32, (BLOCK, 2 * BLOCK), 0)
    col = lax.broadcasted_iota(I32, (BLOCK, 2 * BLOCK), 1)
    diff = row + BLOCK - col
    valid = (diff >= 0) & (diff < BLOCK) & ((col >= BLOCK) | (n > 0))

    group = SWA_Q_HEADS // SWA_KV_HEADS
    for jt in range(SWA_Q_HEADS // 2):
        q_t = rope(q_ref[0, :, jt * LANES:(jt + 1) * LANES].astype(F32), tab_c) * (HEAD_DIM ** -0.5)
        outs = []
        for hh in range(2):
            j = 2 * jt + hh
            g = j // group
            qm = jnp.where(low_head == (hh == 0), q_t, 0.0).astype(BF16)
            s = _dot_nt(qm, k_var[0 if g == hh else 1])
            s = jnp.where(valid, s, NEG_INF)
            sink = sink_ref[j]
            m = jnp.maximum(jnp.max(s, axis=-1, keepdims=True), sink)
            p = jnp.exp(s - m)
            den = jnp.sum(p, axis=-1, keepdims=True) + jnp.exp(sink - m)
            outs.append(_dot(p.astype(BF16), v_var[0 if g == hh else 1]) / den)
        o_ref[0, :, jt * LANES:(jt + 1) * LANES] = jnp.where(low_head, outs[0], outs[1]).astype(o_ref.dtype)


def _swa(proj3, pos3, invf, sinks):
    b, s, _ = proj3.shape
    nb = s // BLOCK
    cur = lambda c: (lambda i, n: (i, n, c))
    prev = lambda c: (lambda i, n: (i, jnp.maximum(n - 1, 0), c))
    return pl.pallas_call(
        _swa_kernel,
        grid=(b, nb),
        in_specs=[
            pl.BlockSpec(memory_space=pltpu.SMEM),
            pl.BlockSpec((1, BLOCK, SWA_WIDTH), cur(COL_Q_SWA // SWA_WIDTH)),
            pl.BlockSpec((1, BLOCK, LANES), cur(COL_K_SWA // LANES)),
            pl.BlockSpec((1, BLOCK, LANES), prev(COL_K_SWA // LANES)),
            pl.BlockSpec((1, BLOCK, LANES), cur(COL_V_SWA // LANES)),
            pl.BlockSpec((1, BLOCK, LANES), prev(COL_V_SWA // LANES)),
            pl.BlockSpec((1, BLOCK, 1), cur(0)),
            pl.BlockSpec((1, BLOCK, 1), prev(0)),
            pl.BlockSpec((1, LANES), lambda i, n: (0, 0)),
        ],
        out_specs=pl.BlockSpec((1, BLOCK, SWA_WIDTH), lambda i, n: (i, n, 0)),
        out_shape=jax.ShapeDtypeStruct((b, s, SWA_WIDTH), BF16),
        compiler_params=_params("arbitrary", "arbitrary"),
        name="swa",
    )(sinks, proj3, proj3, proj3, proj3, proj3, pos3, pos3, invf)


def _sb_kernel(q_ref, k_ref, v_ref, tri_ref, o_ref):
    qb = pl.program_id(2)
    lane = lax.broadcasted_iota(I32, (SB_CHUNK, LANES), 1)
    low_head = lane < HEAD_DIM
    q = q_ref[0].astype(F32) * (HEAD_DIM ** -0.5)
    qms = [jnp.where(low_head == (h == 0), q, 0.0).astype(BF16) for h in range(2)]
    tri = tri_ref[...]
    row = lax.broadcasted_iota(I32, (SB_CHUNK, SB_CHUNK), 0)
    col = lax.broadcasted_iota(I32, (SB_CHUNK, SB_CHUNK), 1)
    strict = col < row

    def sweep(chunk, carry, diagonal):
        off = pl.multiple_of(chunk * SB_CHUNK, SB_CHUNK)
        kc = k_ref[0, pl.ds(off, SB_CHUNK), :]
        vc = v_ref[0, pl.ds(off, SB_CHUNK), :]
        new = []
        for h in range(2):
            acc, later = carry[2 * h], carry[2 * h + 1]
            z = _dot_nt(qms[h], kc)
            log_1m = jnp.minimum(-z, 0.0) - jnp.log(1.0 + jnp.exp(-jnp.abs(z)))
            if diagonal:
                log_1m = jnp.where(strict, log_1m, 0.0)
            l_hi, l_lo = _split_bf16(log_1m)
            incl = _dot(l_hi, tri) + _dot(l_lo, tri)
            attn = jnp.exp(z + incl + later)
            if diagonal:
                attn = jnp.where(strict, attn, 0.0)
            new.append(acc + _dot(attn.astype(BF16), vc))
            new.append(later + jnp.sum(log_1m, axis=-1, keepdims=True))
        return tuple(new)

    zeros = (jnp.zeros((SB_CHUNK, LANES), F32), jnp.zeros((SB_CHUNK, 1), F32))
    carry = sweep(qb, zeros + zeros, True)
    carry = lax.fori_loop(0, qb, lambda i, c: sweep(qb - 1 - i, c, False), carry)
    o_ref[0] = jnp.where(low_head, carry[0], carry[2]).astype(o_ref.dtype)


def _stick_breaking(proj3):
    b, s, _ = proj3.shape
    assert s % SB_CHUNK == 0
    nb = s // SB_CHUNK
    pairs = SB_HEADS // 2
    tri = (np.arange(SB_CHUNK)[:, None] >= np.arange(SB_CHUNK)[None, :])
    tri = jnp.asarray(tri, dtype=BF16)
    return pl.pallas_call(
        _sb_kernel,
        grid=(b, pairs, nb),
        in_specs=[
            pl.BlockSpec((1, SB_CHUNK, LANES), lambda i, p, n: (i, n, COL_Q_SB // LANES + p)),
            pl.BlockSpec((1, s, LANES), lambda i, p, n: (i, 0, COL_K_SB // LANES + p)),
            pl.BlockSpec((1, s, LANES), lambda i, p, n: (i, 0, COL_V_SB // LANES + p)),
            pl.BlockSpec((SB_CHUNK, SB_CHUNK), lambda i, p, n: (0, 0)),
        ],
        out_specs=pl.BlockSpec((1, SB_CHUNK, LANES), lambda i, p, n: (i, n, p)),
        out_shape=jax.ShapeDtypeStruct((b, s, SB_WIDTH), BF16),
        compiler_params=_params("arbitrary", "arbitrary", "arbitrary"),
        name="stick_breaking",
    )(proj3, proj3, proj3, tri)


def _merge_kernel(ya_ref, yb_ref, ga_ref, gb_ref, x_ref, gate_ref, wa_ref, wb_ref, wo_ref, o_ref):
    ma = _dot(ya_ref[...], wa_ref[...])
    mb = _dot(yb_ref[...], wb_ref[...])
    merged = (jax.nn.sigmoid(ga_ref[...].astype(F32)) * ma
              + jax.nn.sigmoid(gb_ref[...].astype(F32)) * mb)
    o_ref[...] = x_ref[...] + gate_ref[0] * _dot(merged.astype(BF16), wo_ref[...])


def _merge(ya, yb, proj, x2, gate, wa, wb, wo, seq, tm=512):
    t, d = x2.shape
    per_batch = lambda i: ((i * tm) // seq, 0, 0)
    return pl.pallas_call(
        _merge_kernel,
        grid=(t // tm,),
        in_specs=[
            pl.BlockSpec((tm, SWA_WIDTH), lambda i: (i, 0)),
            pl.BlockSpec((tm, SB_WIDTH), lambda i: (i, 0)),
            pl.BlockSpec((tm, d), lambda i: (i, COL_GATE_SWA // d)),
            pl.BlockSpec((tm, d), lambda i: (i, COL_GATE_SB // d)),
            pl.BlockSpec((tm, d), lambda i: (i, 0)),
            pl.BlockSpec((1, 1, d), per_batch),
            _resident((SWA_WIDTH, d), lambda i: (0, 0)),
            _resident((SB_WIDTH, d), lambda i: (0, 0)),
            _resident((d, d), lambda i: (0, 0)),
        ],
        out_specs=pl.BlockSpec((tm, d), lambda i: (i, 0)),
        out_shape=jax.ShapeDtypeStruct((t, d), F32),
        compiler_params=_params("arbitrary"),
        name="merge_out",
    )(ya, yb, proj, proj, x2, gate, wa, wb, wo)


def _extract_top(vals, payload, k):
    r = vals.shape[0]
    idx = lax.broadcasted_iota(I32, vals.shape, 0)
    top_v, top_p = [], []
    for _ in range(k):
        m = jnp.max(vals, axis=0, keepdims=True)
        first = jnp.min(jnp.where(vals == m, idx, r), axis=0, keepdims=True)
        hit = idx == first
        top_v.append(m)
        if payload is None:
            top_p.append(first)
        else:
            top_p.append(jnp.sum(jnp.where(hit, payload, 0), axis=0, keepdims=True))
        vals = jnp.where(hit, -jnp.inf, vals)
    return top_v, top_p


def _peer_route_kernel(x_ref, g_ref, scale_ref, shift_ref, wq_ref, sk_ref, h_ref, idx_ref, gate_ref):
    h = _modulated_norm(x_ref[...], g_ref[...], scale_ref[0], shift_ref[0])
    h_ref[...] = h
    q = _dot(h.astype(BF16), wq_ref[...]).astype(BF16)
    idx_rows, gate_rows = [], []
    for head in range(PEER_HEADS):
        halves = []
        for p in range(2):
            c0 = (head * 2 + p) * PEER_HALF
            scores = _dot_nt(sk_ref[head * 2 + p], q[:, c0:c0 + PEER_HALF])
            halves.append(_extract_top(scores, None, PEER_TOPK))
        (s0, i0), (s1, i1) = halves
        s1_all = jnp.concatenate(s1, axis=0)
        i1_all = jnp.concatenate(i1, axis=0)
        cand_s, cand_i = [], []
        for i in range(PEER_TOPK):
            n_j = PEER_TOPK // (i + 1)
            rows = PEER_TOPK if n_j > SUBLANES else SUBLANES
            s_piece = s0[i] + s1_all[0:rows]
            i_piece = i0[i] * PEER_NKEYS + i1_all[0:rows]
            if n_j < rows:
                keep = lax.broadcasted_iota(I32, s_piece.shape, 0) < n_j
                s_piece = jnp.where(keep, s_piece, -jnp.inf)
            cand_s.append(s_piece)
            cand_i.append(i_piece)
        best_s, best_i = _extract_top(jnp.concatenate(cand_s, axis=0),
                                      jnp.concatenate(cand_i, axis=0), PEER_TOPK)
        e = [jnp.exp(s - best_s[0]) for s in best_s]
        den = e[0]
        for t in e[1:]:
            den = den + t
        idx_rows += best_i
        gate_rows += [t / den for t in e]
    idx_ref[...] = jnp.concatenate(idx_rows, axis=0)
    gate_ref[...] = jnp.concatenate(gate_rows, axis=0)


def _peer_route(x2, g, scale, shift, wq, sk, seq, tm=256):
    t, d = x2.shape
    nq = wq.shape[1]
    nsel = PEER_HEADS * PEER_TOPK
    per_batch = lambda i: ((i * tm) // seq, 0, 0)
    return pl.pallas_call(
        _peer_route_kernel,
        grid=(t // tm,),
        in_specs=[
            pl.BlockSpec((tm, d), lambda i: (i, 0)),
            pl.BlockSpec((1, d), lambda i: (0, 0)),
            pl.BlockSpec((1, 1, d), per_batch),
            pl.BlockSpec((1, 1, d), per_batch),
            _resident((d, nq), lambda i: (0, 0)),
            _resident(sk.shape, lambda i: (0, 0, 0)),
        ],
        out_specs=[
            pl.BlockSpec((tm, d), lambda i: (i, 0)),
            pl.BlockSpec((nsel, tm), lambda i: (0, i)),
            pl.BlockSpec((nsel, tm), lambda i: (0, i)),
        ],
        out_shape=[
            jax.ShapeDtypeStruct((t, d), F32),
            jax.ShapeDtypeStruct((nsel, t), I32),
            jax.ShapeDtypeStruct((nsel, t), F32),
        ],
        compiler_params=_params("arbitrary"),
        name="peer_route",
    )(x2, g.reshape(1, d), scale, shift, wq, sk)


def _peer_expert_kernel(idx_ref, idx_next_ref, gate_ref, h_ref, x_ref, mod_ref, tab_ref, o_ref,
                        rows_a, rows_b, sem):
    i = pl.program_id(0)
    nsel = PEER_HEADS * PEER_TOPK
    d = h_ref.shape[1]
    n_sub = PEER_TOK // PEER_SUB
    bufs = ((rows_a, sem.at[0]), (rows_b, sem.at[1]))

    def row_copy(src_idx_ref, flat, dst, dst_sem, r):
        e = src_idx_ref[flat]
        return pltpu.make_async_copy(tab_ref.at[e], dst.at[pl.ds(r, 1)], dst_sem)

    def start_token(src_idx_ref, base, j, dst, dst_sem):
        for kq in range(nsel):
            row_copy(src_idx_ref, base + j * nsel + kq, dst, dst_sem, j * nsel + kq).start(priority=kq % 2)

    def wait_batch(dst, dst_sem):
        pltpu.make_async_copy(tab_ref.at[pl.ds(0, PEER_SUB * nsel), 0], dst, dst_sem).wait()

    def reduce_token(tok, j, src):
        h_row = h_ref[pl.ds(tok, 1), :]
        u_rows = src[j * nsel:(j + 1) * nsel, 0:d]
        act = jnp.sum(u_rows * h_row, axis=-1, keepdims=True)
        gate_col = pltpu.roll(gate_ref[...], (PEER_TOK - tok) & (PEER_TOK - 1), 1)[:, 0:1]
        w = gate_col * (0.5 * act * (1.0 + lax.erf(act * np.float32(np.sqrt(0.5)))))
        v_rows = src[j * nsel:(j + 1) * nsel, d:2 * d]
        out_row = jnp.sum(v_rows * w, axis=0, keepdims=True)
        o_ref[pl.ds(tok, 1), :] = x_ref[pl.ds(tok, 1), :] + mod_ref[0] * out_row

    def batch(s, parity, src_idx_ref, next_base):
        src, src_sem = bufs[parity]
        dst, dst_sem = bufs[1 - parity]
        wait_batch(src, src_sem)
        for j in range(PEER_SUB):
            start_token(src_idx_ref, next_base, j, dst, dst_sem)
            reduce_token(s * PEER_SUB + j, j, src)

    @pl.when(i == 0)
    def _():
        def first(r, _):
            row_copy(idx_ref, r, rows_a, sem.at[0], r).start()
            return 0
        lax.fori_loop(0, PEER_SUB * nsel, first, 0)

    def pair(p, _):
        s = 2 * p
        batch(s, 0, idx_ref, (s + 1) * PEER_SUB * nsel)
        batch(s + 1, 1, idx_ref, (s + 2) * PEER_SUB * nsel)
        return 0

    lax.fori_loop(0, n_sub // 2 - 1, pair, 0)
    batch(n_sub - 2, 0, idx_ref, (n_sub - 1) * PEER_SUB * nsel)
    batch(n_sub - 1, 1, idx_next_ref, 0)

    @pl.when(i == pl.num_programs(0) - 1)
    def _():
        wait_batch(rows_a, sem.at[0])


def _peer_experts(idx_flat, gate_t, h2, x2, gate, table3, seq):
    t, d = x2.shape
    nsel = PEER_HEADS * PEER_TOPK
    n_steps = t // PEER_TOK
    per_batch = lambda i: ((i * PEER_TOK) // seq, 0, 0)
    idx_block = PEER_TOK * nsel
    return pl.pallas_call(
        _peer_expert_kernel,
        grid=(n_steps,),
        in_specs=[
            pl.BlockSpec((idx_block,), lambda i: (i,), memory_space=pltpu.SMEM),
            pl.BlockSpec((idx_block,), lambda i: (jnp.minimum(i + 1, n_steps - 1),),
                         memory_space=pltpu.SMEM),
            pl.BlockSpec((nsel, PEER_TOK), lambda i: (0, i)),
            pl.BlockSpec((PEER_TOK, d), lambda i: (i, 0)),
            pl.BlockSpec((PEER_TOK, d), lambda i: (i, 0)),
            pl.BlockSpec((1, 1, d), per_batch),
            pl.BlockSpec(memory_space=pl.ANY),
        ],
        out_specs=pl.BlockSpec((PEER_TOK, d), lambda i: (i, 0)),
        out_shape=jax.ShapeDtypeStruct((t, d), F32),
        scratch_shapes=[
            pltpu.VMEM((PEER_SUB * nsel, 2 * d), F32),
            pltpu.VMEM((PEER_SUB * nsel, 2 * d), F32),
            pltpu.SemaphoreType.DMA((2,)),
        ],
        compiler_params=_params("arbitrary"),
        name="peer_experts",
    )(idx_flat, idx_flat, gate_t, h2, x2, gate, table3)


def _final_norm_kernel(x_ref, g_ref, o_ref):
    x = x_ref[...]
    ms = jnp.mean(x * x, axis=-1, keepdims=True)
    o_ref[...] = x * lax.rsqrt(ms + NORM_EPS) * g_ref[...]


def _final_norm(x2, g, tm=512):
    t, d = x2.shape
    return pl.pallas_call(
        _final_norm_kernel,
        grid=(t // tm,),
        in_specs=[pl.BlockSpec((tm, d), lambda i: (i, 0)), pl.BlockSpec((1, d), lambda i: (0, 0))],
        out_specs=pl.BlockSpec((tm, d), lambda i: (i, 0)),
        out_shape=jax.ShapeDtypeStruct((t, d), F32),
        compiler_params=_params("arbitrary"),
        name="final_norm",
    )(x2, g.reshape(1, d))


def _reorder_in_cols(w):
    n_qkv = SWA_WIDTH + 2 * SWA_KV_WIDTH + 3 * SB_WIDTH
    return jnp.concatenate([w[:, n_qkv:], w[:, :n_qkv]], axis=1)


def kernel(x, c, positions, ada_w, ada_b, norm1_g, w_in, swa_sinks, w_branch_swa, w_branch_sb,
           w_out, norm2_g, peer_wq, peer_subkeys, peer_u, peer_v, final_g):
    b, s, d = x.shape
    depth = ada_w.shape[0]
    t = b * s
    mod = _ada_mod(c, ada_w, ada_b)
    pos3 = positions.reshape(b, s, 1)
    inv_freq = jnp.power(ROPE_THETA, -jnp.arange(HEAD_DIM // 2, dtype=F32) * (2.0 / HEAD_DIM))
    invf = jnp.tile(inv_freq, LANES // (HEAD_DIM // 2)).reshape(1, LANES)
    x2 = x.reshape(t, d)
    for l in range(depth):
        m = mod[l].reshape(b, N_MOD, 1, d)
        shift1, scale1, gate1, shift2, scale2, gate2 = [m[:, i] for i in range(N_MOD)]
        w_in_l = _reorder_in_cols(w_in[l]).astype(BF16)
        proj = _norm_proj(x2, norm1_g[l], scale1, shift1, w_in_l, s)
        proj3 = proj.reshape(b, s, IN_COLS)
        y_a = _swa(proj3, pos3, invf, swa_sinks[l])
        y_b = _stick_breaking(proj3)
        x2 = _merge(y_a.reshape(t, SWA_WIDTH), y_b.reshape(t, SB_WIDTH), proj, x2, gate1,
                    w_branch_swa[l].astype(BF16), w_branch_sb[l].astype(BF16),
                    w_out[l].astype(BF16), s)
        sk = peer_subkeys[l].reshape(PEER_HEADS * 2, PEER_NKEYS, PEER_HALF).astype(BF16)
        h2, idx_t, gate_t = _peer_route(x2, norm2_g[l], scale2, shift2, peer_wq[l].astype(BF16), sk, s)
        table = jnp.concatenate([peer_u[l], peer_v[l]], axis=1)[:, None, :]
        x2 = _peer_experts(idx_t.T.reshape(-1), gate_t, h2, x2, gate2, table, s)
    return _final_norm(x2, final_g).reshape(b, s, d)
```

```python
import functools

import jax
import jax.numpy as jnp
import numpy as np
from jax import lax
from jax.experimental import pallas as pl
from jax.experimental.pallas import tpu as pltpu

F32 = jnp.float32
BF16 = jnp.bfloat16
I32 = jnp.int32

HEAD_DIM = 64
BLOCK = 128
SWA_Q_HEADS = 8
SWA_KV_HEADS = 2
SB_HEADS = 8
ROPE_THETA = 10000.0
SWA_WIDTH = SWA_Q_HEADS * HEAD_DIM
SWA_KV_WIDTH = SWA_KV_HEADS * HEAD_DIM
SB_WIDTH = SB_HEADS * HEAD_DIM
PEER_HEADS = 8
PEER_NKEYS = 128
PEER_TOPK = 16
PEER_QDIM = 256
PEER_HALF = PEER_QDIM // 2
N_MOD = 6
NORM_EPS = 1e-6
NEG_INF = -1e30

LANES = 128
SUBLANES = 8
VMEM_LIMIT_BYTES = 56 * 1024 * 1024

COL_GATE_SWA = 0
COL_GATE_SB = 1024
COL_Q_SWA = 2048
COL_K_SWA = COL_Q_SWA + SWA_WIDTH
COL_V_SWA = COL_K_SWA + SWA_KV_WIDTH
COL_Q_SB = COL_V_SWA + SWA_KV_WIDTH
COL_K_SB = COL_Q_SB + SB_WIDTH
COL_V_SB = COL_K_SB + SB_WIDTH
IN_COLS = COL_V_SB + SB_WIDTH

SB_CHUNK = 2 * BLOCK
PEER_SUB = 4
PEER_TOK = 128


def _dot(a, b):
    return jnp.dot(a, b, preferred_element_type=F32)


def _dot_nt(a, b):
    return lax.dot_general(a, b, (((1,), (1,)), ((), ())), preferred_element_type=F32)


def _split_bf16(a):
    hi = a.astype(BF16)
    lo = (a - hi.astype(F32)).astype(BF16)
    return hi, lo


def _params(*sem):
    return pltpu.CompilerParams(dimension_semantics=sem, vmem_limit_bytes=VMEM_LIMIT_BYTES)


def _resident(shape, index_map):
    return pl.BlockSpec(shape, index_map, pipeline_mode=pl.Buffered(1))


def _ada_kernel(c_ref, w_ref, b_ref, o_ref):
    c = c_ref[...]
    a = c * jax.nn.sigmoid(c)
    a_hi, a_lo = _split_bf16(a)
    w_hi, w_lo = _split_bf16(w_ref[0])
    acc = _dot(a_hi, w_hi) + _dot(a_hi, w_lo) + _dot(a_lo, w_hi)
    o_ref[0] = acc + b_ref[0]


def _ada_mod(c, ada_w, ada_b):
    depth, d, n = ada_w.shape
    b = c.shape[0]
    tn = n // 4
    return pl.pallas_call(
        _ada_kernel,
        grid=(depth, n // tn),
        in_specs=[
            pl.BlockSpec((b, d), lambda l, j: (0, 0)),
            pl.BlockSpec((1, d, tn), lambda l, j: (l, 0, j)),
            pl.BlockSpec((1, 1, tn), lambda l, j: (l, 0, j)),
        ],
        out_specs=pl.BlockSpec((1, b, tn), lambda l, j: (l, 0, j)),
        out_shape=jax.ShapeDtypeStruct((depth, b, n), F32),
        compiler_params=_params("arbitrary", "arbitrary"),
        name="ada_mod",
    )(c, ada_w, ada_b.reshape(depth, 1, n))


def _modulated_norm(x, g, scale, shift):
    ms = jnp.mean(x * x, axis=-1, keepdims=True)
    y = x * lax.rsqrt(ms + NORM_EPS) * g
    return y * (1.0 + scale) + shift


def _norm_proj_kernel(x_ref, g_ref, scale_ref, shift_ref, w_ref, o_ref, *, col_chunk):
    h = _modulated_norm(x_ref[...], g_ref[...], scale_ref[0], shift_ref[0]).astype(BF16)
    n = o_ref.shape[1]
    for c0 in range(0, n, col_chunk):
        c1 = min(c0 + col_chunk, n)
        o_ref[:, c0:c1] = _dot(h, w_ref[:, c0:c1]).astype(o_ref.dtype)


def _norm_proj(x2, g, scale, shift, w_bf16, seq, tm=512):
    t, d = x2.shape
    n = w_bf16.shape[1]
    per_batch = lambda i: ((i * tm) // seq, 0, 0)
    return pl.pallas_call(
        functools.partial(_norm_proj_kernel, col_chunk=1024),
        grid=(t // tm,),
        in_specs=[
            pl.BlockSpec((tm, d), lambda i: (i, 0)),
            pl.BlockSpec((1, d), lambda i: (0, 0)),
            pl.BlockSpec((1, 1, d), per_batch),
            pl.BlockSpec((1, 1, d), per_batch),
            _resident((d, n), lambda i: (0, 0)),
        ],
        out_specs=pl.BlockSpec((tm, n), lambda i: (i, 0)),
        out_shape=jax.ShapeDtypeStruct((t, n), BF16),
        compiler_params=_params("arbitrary"),
        name="norm_proj",
    )(x2, g.reshape(1, d), scale, shift, w_bf16)


def _swa_kernel(sink_ref, q_ref, kc_ref, kp_ref, vc_ref, vp_ref, pc_ref, pp_ref, invf_ref, o_ref):
    n = pl.program_id(1)
    lane = lax.broadcasted_iota(I32, (BLOCK, LANES), 1)
    first_half = (lane % HEAD_DIM) < (HEAD_DIM // 2)
    low_head = lane < HEAD_DIM
    invf = invf_ref[...]

    def rope_tables(pos_ref):
        ang = pos_ref[0].astype(F32) * invf
        s = jnp.sin(ang)
        return jnp.cos(ang), jnp.where(first_half, -s, s)

    def rope(x, tables):
        c, s_signed = tables
        swapped = jnp.where(first_half, pltpu.roll(x, LANES - HEAD_DIM // 2, 1),
                            pltpu.roll(x, HEAD_DIM // 2, 1))
        return x * c + swapped * s_signed

    tab_c = rope_tables(pc_ref)
    tab_p = rope_tables(pp_ref)
    k = jnp.concatenate([rope(kp_ref[0].astype(F32), tab_p), rope(kc_ref[0].astype(F32), tab_c)], axis=0)
    v = jnp.concatenate([vp_ref[0], vc_ref[0]], axis=0).astype(F32)
    k_var = (k.astype(BF16), pltpu.roll(k, HEAD_DIM, 1).astype(BF16))
    v_var = (v.astype(BF16), pltpu.roll(v, HEAD_DIM, 1).astype(BF16))

    row = lax.broadcasted_iota(I32, (BLOCK, 2 * BLOCK), 0)
    col = lax.broadcasted_iota(I32, (BLOCK, 2 * BLOCK), 1)
    diff = row + BLOCK - col
    valid = (diff >= 0) & (diff < BLOCK) & ((col >= BLOCK) | (n > 0))

    group = SWA_Q_HEADS // SWA_KV_HEADS
    for jt in range(SWA_Q_HEADS // 2):
        q_t = rope(q_ref[0, :, jt * LANES:(jt + 1) * LANES].astype(F32), tab_c) * (HEAD_DIM ** -0.5)
        outs = []
        for hh in range(2):
            j = 2 * jt + hh
            g = j // group
            qm = jnp.where(low_head == (hh == 0), q_t, 0.0).astype(BF16)
            s = _dot_nt(qm, k_var[0 if g == hh else 1])
            s = jnp.where(valid, s, NEG_INF)
            sink = sink_ref[j]
            m = jnp.maximum(jnp.max(s, axis=-1, keepdims=True), sink)
            p = jnp.exp(s - m)
            den = jnp.sum(p, axis=-1, keepdims=True) + jnp.exp(sink - m)
            outs.append(_dot(p.astype(BF16), v_var[0 if g == hh else 1]) / den)
        o_ref[0, :, jt * LANES:(jt + 1) * LANES] = jnp.where(low_head, outs[0], outs[1]).astype(o_ref.dtype)


def _swa(proj3, pos3, invf, sinks):
    b, s, _ = proj3.shape
    nb = s // BLOCK
    cur = lambda c: (lambda i, n: (i, n, c))
    prev = lambda c: (lambda i, n: (i, jnp.maximum(n - 1, 0), c))
    return pl.pallas_call(
        _swa_kernel,
        grid=(b, nb),
        in_specs=[
            pl.BlockSpec(memory_space=pltpu.SMEM),
            pl.BlockSpec((1, BLOCK, SWA_WIDTH), cur(COL_Q_SWA // SWA_WIDTH)),
            pl.BlockSpec((1, BLOCK, LANES), cur(COL_K_SWA // LANES)),
            pl.BlockSpec((1, BLOCK, LANES), prev(COL_K_SWA // LANES)),
            pl.BlockSpec((1, BLOCK, LANES), cur(COL_V_SWA // LANES)),
            pl.BlockSpec((1, BLOCK, LANES), prev(COL_V_SWA // LANES)),
            pl.BlockSpec((1, BLOCK, 1), cur(0)),
            pl.BlockSpec((1, BLOCK, 1), prev(0)),
            pl.BlockSpec((1, LANES), lambda i, n: (0, 0)),
        ],
        out_specs=pl.BlockSpec((1, BLOCK, SWA_WIDTH), lambda i, n: (i, n, 0)),
        out_shape=jax.ShapeDtypeStruct((b, s, SWA_WIDTH), BF16),
        compiler_params=_params("arbitrary", "arbitrary"),
        name="swa",
    )(sinks, proj3, proj3, proj3, proj3, proj3, pos3, pos3, invf)


def _sb_kernel(q_ref, k_ref, v_ref, tri_ref, o_ref):
    qb = pl.program_id(2)
    lane = lax.broadcasted_iota(I32, (SB_CHUNK, LANES), 1)
    low_head = lane < HEAD_DIM
    q = q_ref[0].astype(F32) * (HEAD_DIM ** -0.5)
    qms = [jnp.where(low_head == (h == 0), q, 0.0).astype(BF16) for h in range(2)]
    tri = tri_ref[...]
    row = lax.broadcasted_iota(I32, (SB_CHUNK, SB_CHUNK), 0)
    col = lax.broadcasted_iota(I32, (SB_CHUNK, SB_CHUNK), 1)
    strict = col < row

    def sweep_pair(c0, carry, diagonal):
        c1 = c0 - 1
        exists = c1 >= 0
        off0 = pl.multiple_of(c0 * SB_CHUNK, SB_CHUNK)
        off1 = pl.multiple_of(jnp.maximum(c1, 0) * SB_CHUNK, SB_CHUNK)
        k0 = k_ref[0, pl.ds(off0, SB_CHUNK), :]
        v0 = v_ref[0, pl.ds(off0, SB_CHUNK), :]
        k1 = k_ref[0, pl.ds(off1, SB_CHUNK), :]
        v1 = v_ref[0, pl.ds(off1, SB_CHUNK), :]
        v1 = jnp.where(exists, v1, jnp.zeros_like(v1))
        chains = [(h, kc, vc, diagonal and first) for h in range(2)
                  for kc, vc, first in ((k0, v0, True), (k1, v1, False))]
        zs = [_dot_nt(qms[h], kc) for h, kc, _, _ in chains]
        log_1m = []
        for z, (_, _, _, masked) in zip(zs, chains):
            t = jnp.minimum(-z, 0.0) - jnp.log(1.0 + jnp.exp(-jnp.abs(z)))
            log_1m.append(jnp.where(strict, t, 0.0) if masked else t)
        sums = [jnp.sum(t, axis=-1, keepdims=True) for t in log_1m]
        incl = []
        for t in log_1m:
            l_hi, l_lo = _split_bf16(t)
            incl.append(_dot(l_hi, tri) + _dot(l_lo, tri))
        new = []
        for h in range(2):
            acc, later = carry[2 * h], carry[2 * h + 1]
            for n, later_n in ((2 * h, later), (2 * h + 1, later + sums[2 * h])):
                attn = jnp.exp(zs[n] + incl[n] + later_n)
                if chains[n][3]:
                    attn = jnp.where(strict, attn, 0.0)
                acc = acc + _dot(attn.astype(BF16), chains[n][2])
            new.append(acc)
            new.append(later + sums[2 * h] + sums[2 * h + 1])
        return tuple(new)

    zeros = (jnp.zeros((SB_CHUNK, LANES), F32), jnp.zeros((SB_CHUNK, 1), F32))
    carry = sweep_pair(qb, zeros + zeros, True)
    n_pairs = lax.shift_right_logical(qb + 2, 1)
    carry = lax.fori_loop(1, n_pairs, lambda i, c: sweep_pair(qb - 2 * i, c, False), carry)
    o_ref[0] = jnp.where(low_head, carry[0], carry[2]).astype(o_ref.dtype)


def _stick_breaking(proj3):
    b, s, _ = proj3.shape
    assert s % SB_CHUNK == 0
    nb = s // SB_CHUNK
    pairs = SB_HEADS // 2
    tri = (np.arange(SB_CHUNK)[:, None] >= np.arange(SB_CHUNK)[None, :])
    tri = jnp.asarray(tri, dtype=BF16)
    return pl.pallas_call(
        _sb_kernel,
        grid=(b, pairs, nb),
        in_specs=[
            pl.BlockSpec((1, SB_CHUNK, LANES), lambda i, p, n: (i, n, COL_Q_SB // LANES + p)),
            pl.BlockSpec((1, s, LANES), lambda i, p, n: (i, 0, COL_K_SB // LANES + p)),
            pl.BlockSpec((1, s, LANES), lambda i, p, n: (i, 0, COL_V_SB // LANES + p)),
            pl.BlockSpec((SB_CHUNK, SB_CHUNK), lambda i, p, n: (0, 0)),
        ],
        out_specs=pl.BlockSpec((1, SB_CHUNK, LANES), lambda i, p, n: (i, n, p)),
        out_shape=jax.ShapeDtypeStruct((b, s, SB_WIDTH), BF16),
        compiler_params=_params("arbitrary", "arbitrary", "arbitrary"),
        name="stick_breaking",
    )(proj3, proj3, proj3, tri)


def _merge_kernel(ya_ref, yb_ref, ga_ref, gb_ref, x_ref, gate_ref, wa_ref, wb_ref, wo_ref, o_ref):
    ma = _dot(ya_ref[...], wa_ref[...])
    mb = _dot(yb_ref[...], wb_ref[...])
    merged = (jax.nn.sigmoid(ga_ref[...].astype(F32)) * ma
              + jax.nn.sigmoid(gb_ref[...].astype(F32)) * mb)
    o_ref[...] = x_ref[...] + gate_ref[0] * _dot(merged.astype(BF16), wo_ref[...])


def _merge(ya, yb, proj, x2, gate, wa, wb, wo, seq, tm=512):
    t, d = x2.shape
    per_batch = lambda i: ((i * tm) // seq, 0, 0)
    return pl.pallas_call(
        _merge_kernel,
        grid=(t // tm,),
        in_specs=[
            pl.BlockSpec((tm, SWA_WIDTH), lambda i: (i, 0)),
            pl.BlockSpec((tm, SB_WIDTH), lambda i: (i, 0)),
            pl.BlockSpec((tm, d), lambda i: (i, COL_GATE_SWA // d)),
            pl.BlockSpec((tm, d), lambda i: (i, COL_GATE_SB // d)),
            pl.BlockSpec((tm, d), lambda i: (i, 0)),
            pl.BlockSpec((1, 1, d), per_batch),
            _resident((SWA_WIDTH, d), lambda i: (0, 0)),
            _resident((SB_WIDTH, d), lambda i: (0, 0)),
            _resident((d, d), lambda i: (0, 0)),
        ],
        out_specs=pl.BlockSpec((tm, d), lambda i: (i, 0)),
        out_shape=jax.ShapeDtypeStruct((t, d), F32),
        compiler_params=_params("arbitrary"),
        name="merge_out",
    )(ya, yb, proj, proj, x2, gate, wa, wb, wo)


def _extract_top(vals, payload, k):
    r = vals.shape[0]
    idx = lax.broadcasted_iota(I32, vals.shape, 0)
    top_v, top_p = [], []
    for _ in range(k):
        m = jnp.max(vals, axis=0, keepdims=True)
        first = jnp.min(jnp.where(vals == m, idx, r), axis=0, keepdims=True)
        hit = idx == first
        top_v.append(m)
        if payload is None:
            top_p.append(first)
        else:
            top_p.append(jnp.sum(jnp.where(hit, payload, 0), axis=0, keepdims=True))
        vals = jnp.where(hit, -jnp.inf, vals)
    return top_v, top_p


def _peer_route_kernel(x_ref, g_ref, scale_ref, shift_ref, wq_ref, sk_ref, h_ref, idx_ref, gate_ref):
    h = _modulated_norm(x_ref[...], g_ref[...], scale_ref[0], shift_ref[0])
    h_ref[...] = h
    q = _dot(h.astype(BF16), wq_ref[...]).astype(BF16)
    idx_rows, gate_rows = [], []
    for head in range(PEER_HEADS):
        halves = []
        for p in range(2):
            c0 = (head * 2 + p) * PEER_HALF
            scores = _dot_nt(sk_ref[head * 2 + p], q[:, c0:c0 + PEER_HALF])
            halves.append(_extract_top(scores, None, PEER_TOPK))
        (s0, i0), (s1, i1) = halves
        s1_all = jnp.concatenate(s1, axis=0)
        i1_all = jnp.concatenate(i1, axis=0)
        cand_s, cand_i = [], []
        for i in range(PEER_TOPK):
            n_j = PEER_TOPK // (i + 1)
            rows = PEER_TOPK if n_j > SUBLANES else SUBLANES
            s_piece = s0[i] + s1_all[0:rows]
            i_piece = i0[i] * PEER_NKEYS + i1_all[0:rows]
            if n_j < rows:
                keep = lax.broadcasted_iota(I32, s_piece.shape, 0) < n_j
                s_piece = jnp.where(keep, s_piece, -jnp.inf)
            cand_s.append(s_piece)
            cand_i.append(i_piece)
        best_s, best_i = _extract_top(jnp.concatenate(cand_s, axis=0),
                                      jnp.concatenate(cand_i, axis=0), PEER_TOPK)
        e = [jnp.exp(s - best_s[0]) for s in best_s]
        den = e[0]
        for t in e[1:]:
            den = den + t
        idx_rows += best_i
        gate_rows += [t / den for t in e]
    idx_ref[...] = jnp.concatenate(idx_rows, axis=0)
    gate_ref[...] = jnp.concatenate(gate_rows, axis=0)


def _peer_route(x2, g, scale, shift, wq, sk, seq, tm=256):
    t, d = x2.shape
    nq = wq.shape[1]
    nsel = PEER_HEADS * PEER_TOPK
    per_batch = lambda i: ((i * tm) // seq, 0, 0)
    return pl.pallas_call(
        _peer_route_kernel,
        grid=(t // tm,),
        in_specs=[
            pl.BlockSpec((tm, d), lambda i: (i, 0)),
            pl.BlockSpec((1, d), lambda i: (0, 0)),
            pl.BlockSpec((1, 1, d), per_batch),
            pl.BlockSpec((1, 1, d), per_batch),
            _resident((d, nq), lambda i: (0, 0)),
            _resident(sk.shape, lambda i: (0, 0, 0)),
        ],
        out_specs=[
            pl.BlockSpec((tm, d), lambda i: (i, 0)),
            pl.BlockSpec((nsel, tm), lambda i: (0, i)),
            pl.BlockSpec((nsel, tm), lambda i: (0, i)),
        ],
        out_shape=[
            jax.ShapeDtypeStruct((t, d), F32),
            jax.ShapeDtypeStruct((nsel, t), I32),
            jax.ShapeDtypeStruct((nsel, t), F32),
        ],
        compiler_params=_params("arbitrary"),
        name="peer_route",
    )(x2, g.reshape(1, d), scale, shift, wq, sk)


def _peer_expert_kernel(idx_ref, idx_next_ref, gate_ref, h_ref, x_ref, mod_ref, tab_ref, o_ref,
                        rows_a, rows_b, sem):
    i = pl.program_id(0)
    nsel = PEER_HEADS * PEER_TOPK
    d = h_ref.shape[1]
    n_sub = PEER_TOK // PEER_SUB
    bufs = ((rows_a, sem.at[0]), (rows_b, sem.at[1]))

    def row_copy(src_idx_ref, flat, dst, dst_sem, r):
        e = src_idx_ref[flat]
        return pltpu.make_async_copy(tab_ref.at[e], dst.at[pl.ds(r, 1)], dst_sem)

    def start_token(src_idx_ref, base, j, dst, dst_sem):
        for kq in range(nsel):
            row_copy(src_idx_ref, base + j * nsel + kq, dst, dst_sem, j * nsel + kq).start(priority=kq % 2)

    def wait_batch(dst, dst_sem):
        pltpu.make_async_copy(tab_ref.at[pl.ds(0, PEER_SUB * nsel), 0], dst, dst_sem).wait()

    def reduce_token(tok, j, src):
        h_row = h_ref[pl.ds(tok, 1), :]
        words = src[j * nsel:(j + 1) * nsel, :]
        u_rows = lax.bitcast_convert_type(words & jnp.uint32(0xFFFF0000), F32)
        act = jnp.sum(u_rows * h_row, axis=-1, keepdims=True)
        gate_col = pltpu.roll(gate_ref[...], (PEER_TOK - tok) & (PEER_TOK - 1), 1)[:, 0:1]
        w = gate_col * (0.5 * act * (1.0 + lax.erf(act * np.float32(np.sqrt(0.5)))))
        v_rows = lax.bitcast_convert_type(words << 16, F32)
        out_row = jnp.sum(v_rows * w, axis=0, keepdims=True)
        o_ref[pl.ds(tok, 1), :] = x_ref[pl.ds(tok, 1), :] + mod_ref[0] * out_row

    def batch(s, parity, src_idx_ref, next_base):
        src, src_sem = bufs[parity]
        dst, dst_sem = bufs[1 - parity]
        wait_batch(src, src_sem)
        for j in range(PEER_SUB):
            start_token(src_idx_ref, next_base, j, dst, dst_sem)
            reduce_token(s * PEER_SUB + j, j, src)

    @pl.when(i == 0)
    def _():
        def first(r, _):
            row_copy(idx_ref, r, rows_a, sem.at[0], r).start()
            return 0
        lax.fori_loop(0, PEER_SUB * nsel, first, 0)

    def pair(p, _):
        s = 2 * p
        batch(s, 0, idx_ref, (s + 1) * PEER_SUB * nsel)
        batch(s + 1, 1, idx_ref, (s + 2) * PEER_SUB * nsel)
        return 0

    lax.fori_loop(0, n_sub // 2 - 1, pair, 0)
    batch(n_sub - 2, 0, idx_ref, (n_sub - 1) * PEER_SUB * nsel)
    batch(n_sub - 1, 1, idx_next_ref, 0)

    @pl.when(i == pl.num_programs(0) - 1)
    def _():
        wait_batch(rows_a, sem.at[0])


def _peer_experts(idx_flat, gate_t, h2, x2, gate, table3, seq):
    t, d = x2.shape
    nsel = PEER_HEADS * PEER_TOPK
    n_steps = t // PEER_TOK
    per_batch = lambda i: ((i * PEER_TOK) // seq, 0, 0)
    idx_block = PEER_TOK * nsel
    return pl.pallas_call(
        _peer_expert_kernel,
        grid=(n_steps,),
        in_specs=[
            pl.BlockSpec((idx_block,), lambda i: (i,), memory_space=pltpu.SMEM),
            pl.BlockSpec((idx_block,), lambda i: (jnp.minimum(i + 1, n_steps - 1),),
                         memory_space=pltpu.SMEM),
            pl.BlockSpec((nsel, PEER_TOK), lambda i: (0, i)),
            pl.BlockSpec((PEER_TOK, d), lambda i: (i, 0)),
            pl.BlockSpec((PEER_TOK, d), lambda i: (i, 0)),
            pl.BlockSpec((1, 1, d), per_batch),
            pl.BlockSpec(memory_space=pl.ANY),
        ],
        out_specs=pl.BlockSpec((PEER_TOK, d), lambda i: (i, 0)),
        out_shape=jax.ShapeDtypeStruct((t, d), F32),
        scratch_shapes=[
            pltpu.VMEM((PEER_SUB * nsel, d), jnp.uint32),
            pltpu.VMEM((PEER_SUB * nsel, d), jnp.uint32),
            pltpu.SemaphoreType.DMA((2,)),
        ],
        compiler_params=_params("arbitrary"),
        name="peer_experts",
    )(idx_flat, idx_flat, gate_t, h2, x2, gate, table3)


def _final_norm_kernel(x_ref, g_ref, o_ref):
    x = x_ref[...]
    ms = jnp.mean(x * x, axis=-1, keepdims=True)
    o_ref[...] = x * lax.rsqrt(ms + NORM_EPS) * g_ref[...]


def _final_norm(x2, g, tm=512):
    t, d = x2.shape
    return pl.pallas_call(
        _final_norm_kernel,
        grid=(t // tm,),
        in_specs=[pl.BlockSpec((tm, d), lambda i: (i, 0)), pl.BlockSpec((1, d), lambda i: (0, 0))],
        out_specs=pl.BlockSpec((tm, d), lambda i: (i, 0)),
        out_shape=jax.ShapeDtypeStruct((t, d), F32),
        compiler_params=_params("arbitrary"),
        name="final_norm",
    )(x2, g.reshape(1, d))


def _pack_expert_rows(u, v):
    hi = lax.bitcast_convert_type(u.astype(BF16), jnp.uint16).astype(jnp.uint32)
    lo = lax.bitcast_convert_type(v.astype(BF16), jnp.uint16).astype(jnp.uint32)
    return (hi << 16) | lo


def _reorder_in_cols(w):
    n_qkv = SWA_WIDTH + 2 * SWA_KV_WIDTH + 3 * SB_WIDTH
    return jnp.concatenate([w[:, n_qkv:], w[:, :n_qkv]], axis=1)


def kernel(x, c, positions, ada_w, ada_b, norm1_g, w_in, swa_sinks, w_branch_swa, w_branch_sb,
           w_out, norm2_g, peer_wq, peer_subkeys, peer_u, peer_v, final_g):
    b, s, d = x.shape
    depth = ada_w.shape[0]
    t = b * s
    mod = _ada_mod(c, ada_w, ada_b)
    pos3 = positions.reshape(b, s, 1)
    inv_freq = jnp.power(ROPE_THETA, -jnp.arange(HEAD_DIM // 2, dtype=F32) * (2.0 / HEAD_DIM))
    invf = jnp.tile(inv_freq, LANES // (HEAD_DIM // 2)).reshape(1, LANES)
    x2 = x.reshape(t, d)
    for l in range(depth):
        m = mod[l].reshape(b, N_MOD, 1, d)
        shift1, scale1, gate1, shift2, scale2, gate2 = [m[:, i] for i in range(N_MOD)]
        w_in_l = _reorder_in_cols(w_in[l]).astype(BF16)
        proj = _norm_proj(x2, norm1_g[l], scale1, shift1, w_in_l, s)
        proj3 = proj.reshape(b, s, IN_COLS)
        y_a = _swa(proj3, pos3, invf, swa_sinks[l])
        y_b = _stick_breaking(proj3)
        x2 = _merge(y_a.reshape(t, SWA_WIDTH), y_b.reshape(t, SB_WIDTH), proj, x2, gate1,
                    w_branch_swa[l].astype(BF16), w_branch_sb[l].astype(BF16),
                    w_out[l].astype(BF16), s)
        sk = peer_subkeys[l].reshape(PEER_HEADS * 2, PEER_NKEYS, PEER_HALF).astype(BF16)
        h2, idx_t, gate_t = _peer_route(x2, norm2_g[l], scale2, shift2, peer_wq[l].astype(BF16), sk, s)
        table = _pack_expert_rows(peer_u[l], peer_v[l])[:, None, :]
        x2 = _peer_experts(idx_t.T.reshape(-1), gate_t, h2, x2, gate2, table, s)
    return _final_norm(x2, final_g).reshape(b, s, d)
```

```python
import functools

import jax
import jax.numpy as jnp
import numpy as np
from jax import lax
from jax.experimental import pallas as pl
from jax.experimental.pallas import tpu as pltpu
from jax.experimental.pallas import tpu_sc as plsc

F32 = jnp.float32
BF16 = jnp.bfloat16
I32 = jnp.int32

HEAD_DIM = 64
BLOCK = 128
SWA_Q_HEADS = 8
SWA_KV_HEADS = 2
SB_HEADS = 8
ROPE_THETA = 10000.0
SWA_WIDTH = SWA_Q_HEADS * HEAD_DIM
SWA_KV_WIDTH = SWA_KV_HEADS * HEAD_DIM
SB_WIDTH = SB_HEADS * HEAD_DIM
PEER_HEADS = 8
PEER_NKEYS = 128
PEER_TOPK = 16
PEER_QDIM = 256
PEER_HALF = PEER_QDIM // 2
N_MOD = 6
NORM_EPS = 1e-6
NEG_INF = -1e30

LANES = 128
SUBLANES = 8
VMEM_LIMIT_BYTES = 56 * 1024 * 1024

COL_GATE_SWA = 0
COL_GATE_SB = 1024
COL_Q_SWA = 2048
COL_K_SWA = COL_Q_SWA + SWA_WIDTH
COL_V_SWA = COL_K_SWA + SWA_KV_WIDTH
COL_Q_SB = COL_V_SWA + SWA_KV_WIDTH
COL_K_SB = COL_Q_SB + SB_WIDTH
COL_V_SB = COL_K_SB + SB_WIDTH
IN_COLS = COL_V_SB + SB_WIDTH

SB_CHUNK = 2 * BLOCK
PEER_SUB = 4
PEER_TOK = 128
PEER_STAGE_TOK = 8
PEER_TC_SHARE = 0.2
PEER_SC_CHUNKS = 4

SC_CORES = 2
SC_SUBCORES = 16
SC_WINDOW = 64


def _dot(a, b):
    return jnp.dot(a, b, preferred_element_type=F32)


def _dot_nt(a, b):
    return lax.dot_general(a, b, (((1,), (1,)), ((), ())), preferred_element_type=F32)


def _split_bf16(a):
    hi = a.astype(BF16)
    lo = (a - hi.astype(F32)).astype(BF16)
    return hi, lo


def _params(*sem):
    return pltpu.CompilerParams(dimension_semantics=sem, vmem_limit_bytes=VMEM_LIMIT_BYTES)


def _resident(shape, index_map):
    return pl.BlockSpec(shape, index_map, pipeline_mode=pl.Buffered(1))


def _ada_kernel(c_ref, w_ref, b_ref, o_ref):
    c = c_ref[...]
    a = c * jax.nn.sigmoid(c)
    a_hi, a_lo = _split_bf16(a)
    w_hi, w_lo = _split_bf16(w_ref[0])
    acc = _dot(a_hi, w_hi) + _dot(a_hi, w_lo) + _dot(a_lo, w_hi)
    o_ref[0] = acc + b_ref[0]


def _ada_mod(c, ada_w, ada_b):
    depth, d, n = ada_w.shape
    b = c.shape[0]
    tn = n // 4
    return pl.pallas_call(
        _ada_kernel,
        grid=(depth, n // tn),
        in_specs=[
            pl.BlockSpec((b, d), lambda l, j: (0, 0)),
            pl.BlockSpec((1, d, tn), lambda l, j: (l, 0, j)),
            pl.BlockSpec((1, 1, tn), lambda l, j: (l, 0, j)),
        ],
        out_specs=pl.BlockSpec((1, b, tn), lambda l, j: (l, 0, j)),
        out_shape=jax.ShapeDtypeStruct((depth, b, n), F32),
        compiler_params=_params("arbitrary", "arbitrary"),
        name="ada_mod",
    )(c, ada_w, ada_b.reshape(depth, 1, n))


def _modulated_norm(x, g, scale, shift):
    ms = jnp.mean(x * x, axis=-1, keepdims=True)
    y = x * lax.rsqrt(ms + NORM_EPS) * g
    return y * (1.0 + scale) + shift


def _norm_proj_kernel(x_ref, g_ref, scale_ref, shift_ref, w_ref, o_ref, *, col_chunk):
    h = _modulated_norm(x_ref[...], g_ref[...], scale_ref[0], shift_ref[0]).astype(BF16)
    n = o_ref.shape[1]
    for c0 in range(0, n, col_chunk):
        c1 = min(c0 + col_chunk, n)
        o_ref[:, c0:c1] = _dot(h, w_ref[:, c0:c1]).astype(o_ref.dtype)


def _norm_proj(x2, g, scale, shift, w_bf16, seq, tm=512):
    t, d = x2.shape
    n = w_bf16.shape[1]
    per_batch = lambda i: ((i * tm) // seq, 0, 0)
    return pl.pallas_call(
        functools.partial(_norm_proj_kernel, col_chunk=1024),
        grid=(t // tm,),
        in_specs=[
            pl.BlockSpec((tm, d), lambda i: (i, 0)),
            pl.BlockSpec((1, d), lambda i: (0, 0)),
            pl.BlockSpec((1, 1, d), per_batch),
            pl.BlockSpec((1, 1, d), per_batch),
            _resident((d, n), lambda i: (0, 0)),
        ],
        out_specs=pl.BlockSpec((tm, n), lambda i: (i, 0)),
        out_shape=jax.ShapeDtypeStruct((t, n), BF16),
        compiler_params=_params("arbitrary"),
        name="norm_proj",
    )(x2, g.reshape(1, d), scale, shift, w_bf16)


def _swa_kernel(sink_ref, q_ref, kc_ref, kp_ref, vc_ref, vp_ref, pc_ref, pp_ref, invf_ref, o_ref):
    n = pl.program_id(1)
    lane = lax.broadcasted_iota(I32, (BLOCK, LANES), 1)
    first_half = (lane % HEAD_DIM) < (HEAD_DIM // 2)
    low_head = lane < HEAD_DIM
    invf = invf_ref[...]

    def rope_tables(pos_ref):
        ang = pos_ref[0].astype(F32) * invf
        s = jnp.sin(ang)
        return jnp.cos(ang), jnp.where(first_half, -s, s)

    def rope(x, tables):
        c, s_signed = tables
        swapped = jnp.where(first_half, pltpu.roll(x, LANES - HEAD_DIM // 2, 1),
                            pltpu.roll(x, HEAD_DIM // 2, 1))
        return x * c + swapped * s_signed

    tab_c = rope_tables(pc_ref)
    tab_p = rope_tables(pp_ref)
    k = jnp.concatenate([rope(kp_ref[0].astype(F32), tab_p), rope(kc_ref[0].astype(F32), tab_c)], axis=0)
    v = jnp.concatenate([vp_ref[0], vc_ref[0]], axis=0).astype(F32)
    k_var = (k.astype(BF16), pltpu.roll(k, HEAD_DIM, 1).astype(BF16))
    v_var = (v.astype(BF16), pltpu.roll(v, HEAD_DIM, 1).astype(BF16))

    row = lax.broadcasted_iota(I32, (BLOCK, 2 * BLOCK), 0)
    col = lax.broadcasted_iota(I32, (BLOCK, 2 * BLOCK), 1)
    diff = row + BLOCK - col
    valid = (diff >= 0) & (diff < BLOCK) & ((col >= BLOCK) | (n > 0))

    group = SWA_Q_HEADS // SWA_KV_HEADS
    for jt in range(SWA_Q_HEADS // 2):
        q_t = rope(q_ref[0, :, jt * LANES:(jt + 1) * LANES].astype(F32), tab_c) * (HEAD_DIM ** -0.5)
        outs = []
        for hh in range(2):
            j = 2 * jt + hh
            g = j // group
            qm = jnp.where(low_head == (hh == 0), q_t, 0.0).astype(BF16)
            s = _dot_nt(qm, k_var[0 if g == hh else 1])
            s = jnp.where(valid, s, NEG_INF)
            sink = sink_ref[j]
            m = jnp.maximum(jnp.max(s, axis=-1, keepdims=True), sink)
            p = jnp.exp(s - m)
            den = jnp.sum(p, axis=-1, keepdims=True) + jnp.exp(sink - m)
            outs.append(_dot(p.astype(BF16), v_var[0 if g == hh else 1]) / den)
        o_ref[0, :, jt * LANES:(jt + 1) * LANES] = jnp.where(low_head, outs[0], outs[1]).astype(o_ref.dtype)


def _swa(proj3, pos3, invf, sinks):
    b, s, _ = proj3.shape
    nb = s // BLOCK
    cur = lambda c: (lambda i, n: (i, n, c))
    prev = lambda c: (lambda i, n: (i, jnp.maximum(n - 1, 0), c))
    return pl.pallas_call(
        _swa_kernel,
        grid=(b, nb),
        in_specs=[
            pl.BlockSpec(memory_space=pltpu.SMEM),
            pl.BlockSpec((1, BLOCK, SWA_WIDTH), cur(COL_Q_SWA // SWA_WIDTH)),
            pl.BlockSpec((1, BLOCK, LANES), cur(COL_K_SWA // LANES)),
            pl.BlockSpec((1, BLOCK, LANES), prev(COL_K_SWA // LANES)),
            pl.BlockSpec((1, BLOCK, LANES), cur(COL_V_SWA // LANES)),
            pl.BlockSpec((1, BLOCK, LANES), prev(COL_V_SWA // LANES)),
            pl.BlockSpec((1, BLOCK, 1), cur(0)),
            pl.BlockSpec((1, BLOCK, 1), prev(0)),
            pl.BlockSpec((1, LANES), lambda i, n: (0, 0)),
        ],
        out_specs=pl.BlockSpec((1, BLOCK, SWA_WIDTH), lambda i, n: (i, n, 0)),
        out_shape=jax.ShapeDtypeStruct((b, s, SWA_WIDTH), BF16),
        compiler_params=_params("arbitrary", "arbitrary"),
        name="swa",
    )(sinks, proj3, proj3, proj3, proj3, proj3, pos3, pos3, invf)


def _sb_kernel(q_ref, k_ref, v_ref, tri_ref, o_ref):
    qb = pl.program_id(2)
    lane = lax.broadcasted_iota(I32, (SB_CHUNK, LANES), 1)
    low_head = lane < HEAD_DIM
    q = q_ref[0].astype(F32) * (HEAD_DIM ** -0.5)
    qms = [jnp.where(low_head == (h == 0), q, 0.0).astype(BF16) for h in range(2)]
    tri = tri_ref[...]
    row = lax.broadcasted_iota(I32, (SB_CHUNK, SB_CHUNK), 0)
    col = lax.broadcasted_iota(I32, (SB_CHUNK, SB_CHUNK), 1)
    strict = col < row

    def sweep_pair(c0, carry, diagonal):
        c1 = c0 - 1
        exists = c1 >= 0
        off0 = pl.multiple_of(c0 * SB_CHUNK, SB_CHUNK)
        off1 = pl.multiple_of(jnp.maximum(c1, 0) * SB_CHUNK, SB_CHUNK)
        k0 = k_ref[0, pl.ds(off0, SB_CHUNK), :]
        v0 = v_ref[0, pl.ds(off0, SB_CHUNK), :]
        k1 = k_ref[0, pl.ds(off1, SB_CHUNK), :]
        v1 = v_ref[0, pl.ds(off1, SB_CHUNK), :]
        v1 = jnp.where(exists, v1, jnp.zeros_like(v1))
        chains = [(h, kc, vc, diagonal and first) for h in range(2)
                  for kc, vc, first in ((k0, v0, True), (k1, v1, False))]
        zs = [_dot_nt(qms[h], kc) for h, kc, _, _ in chains]
        log_1m = []
        for z, (_, _, _, masked) in zip(zs, chains):
            t = jnp.minimum(-z, 0.0) - jnp.log(1.0 + jnp.exp(-jnp.abs(z)))
            log_1m.append(jnp.where(strict, t, 0.0) if masked else t)
        sums = [jnp.sum(t, axis=-1, keepdims=True) for t in log_1m]
        incl = []
        for t in log_1m:
            l_hi, l_lo = _split_bf16(t)
            incl.append(_dot(l_hi, tri) + _dot(l_lo, tri))
        new = []
        for h in range(2):
            acc, later = carry[2 * h], carry[2 * h + 1]
            for n, later_n in ((2 * h, later), (2 * h + 1, later + sums[2 * h])):
                attn = jnp.exp(zs[n] + incl[n] + later_n)
                if chains[n][3]:
                    attn = jnp.where(strict, attn, 0.0)
                acc = acc + _dot(attn.astype(BF16), chains[n][2])
            new.append(acc)
            new.append(later + sums[2 * h] + sums[2 * h + 1])
        return tuple(new)

    zeros = (jnp.zeros((SB_CHUNK, LANES), F32), jnp.zeros((SB_CHUNK, 1), F32))
    carry = sweep_pair(qb, zeros + zeros, True)
    n_pairs = lax.shift_right_logical(qb + 2, 1)
    carry = lax.fori_loop(1, n_pairs, lambda i, c: sweep_pair(qb - 2 * i, c, False), carry)
    o_ref[0] = jnp.where(low_head, carry[0], carry[2]).astype(o_ref.dtype)


def _stick_breaking(proj3):
    b, s, _ = proj3.shape
    assert s % SB_CHUNK == 0
    nb = s // SB_CHUNK
    pairs = SB_HEADS // 2
    tri = (np.arange(SB_CHUNK)[:, None] >= np.arange(SB_CHUNK)[None, :])
    tri = jnp.asarray(tri, dtype=BF16)
    return pl.pallas_call(
        _sb_kernel,
        grid=(b, pairs, nb),
        in_specs=[
            pl.BlockSpec((1, SB_CHUNK, LANES), lambda i, p, n: (i, n, COL_Q_SB // LANES + p)),
            pl.BlockSpec((1, s, LANES), lambda i, p, n: (i, 0, COL_K_SB // LANES + p)),
            pl.BlockSpec((1, s, LANES), lambda i, p, n: (i, 0, COL_V_SB // LANES + p)),
            pl.BlockSpec((SB_CHUNK, SB_CHUNK), lambda i, p, n: (0, 0)),
        ],
        out_specs=pl.BlockSpec((1, SB_CHUNK, LANES), lambda i, p, n: (i, n, p)),
        out_shape=jax.ShapeDtypeStruct((b, s, SB_WIDTH), BF16),
        compiler_params=_params("arbitrary", "arbitrary", "arbitrary"),
        name="stick_breaking",
    )(proj3, proj3, proj3, tri)


def _merge_kernel(ya_ref, yb_ref, ga_ref, gb_ref, x_ref, gate_ref, wa_ref, wb_ref, wo_ref, o_ref):
    ma = _dot(ya_ref[...], wa_ref[...])
    mb = _dot(yb_ref[...], wb_ref[...])
    merged = (jax.nn.sigmoid(ga_ref[...].astype(F32)) * ma
              + jax.nn.sigmoid(gb_ref[...].astype(F32)) * mb)
    o_ref[...] = x_ref[...] + gate_ref[0] * _dot(merged.astype(BF16), wo_ref[...])


def _merge(ya, yb, proj, x2, gate, wa, wb, wo, seq, tm=512):
    t, d = x2.shape
    per_batch = lambda i: ((i * tm) // seq, 0, 0)
    return pl.pallas_call(
        _merge_kernel,
        grid=(t // tm,),
        in_specs=[
            pl.BlockSpec((tm, SWA_WIDTH), lambda i: (i, 0)),
            pl.BlockSpec((tm, SB_WIDTH), lambda i: (i, 0)),
            pl.BlockSpec((tm, d), lambda i: (i, COL_GATE_SWA // d)),
            pl.BlockSpec((tm, d), lambda i: (i, COL_GATE_SB // d)),
            pl.BlockSpec((tm, d), lambda i: (i, 0)),
            pl.BlockSpec((1, 1, d), per_batch),
            _resident((SWA_WIDTH, d), lambda i: (0, 0)),
            _resident((SB_WIDTH, d), lambda i: (0, 0)),
            _resident((d, d), lambda i: (0, 0)),
        ],
        out_specs=pl.BlockSpec((tm, d), lambda i: (i, 0)),
        out_shape=jax.ShapeDtypeStruct((t, d), F32),
        compiler_params=_params("arbitrary"),
        name="merge_out",
    )(ya, yb, proj, proj, x2, gate, wa, wb, wo)


def _extract_top(vals, payload, k):
    r = vals.shape[0]
    idx = lax.broadcasted_iota(I32, vals.shape, 0)
    top_v, top_p = [], []
    for _ in range(k):
        m = jnp.max(vals, axis=0, keepdims=True)
        first = jnp.min(jnp.where(vals == m, idx, r), axis=0, keepdims=True)
        hit = idx == first
        top_v.append(m)
        if payload is None:
            top_p.append(first)
        else:
            top_p.append(jnp.sum(jnp.where(hit, payload, 0), axis=0, keepdims=True))
        vals = jnp.where(hit, -jnp.inf, vals)
    return top_v, top_p


def _peer_route_kernel(x_ref, g_ref, scale_ref, shift_ref, wq_ref, sk_ref, h_ref, idx_ref, gate_ref):
    h = _modulated_norm(x_ref[...], g_ref[...], scale_ref[0], shift_ref[0])
    h_ref[...] = h
    q = _dot(h.astype(BF16), wq_ref[...]).astype(BF16)
    idx_rows, gate_rows = [], []
    for head in range(PEER_HEADS):
        halves = []
        for p in range(2):
            c0 = (head * 2 + p) * PEER_HALF
            scores = _dot_nt(sk_ref[head * 2 + p], q[:, c0:c0 + PEER_HALF])
            halves.append(_extract_top(scores, None, PEER_TOPK))
        (s0, i0), (s1, i1) = halves
        s1_all = jnp.concatenate(s1, axis=0)
        i1_all = jnp.concatenate(i1, axis=0)
        cand_s, cand_i = [], []
        for i in range(PEER_TOPK):
            n_j = PEER_TOPK // (i + 1)
            rows = PEER_TOPK if n_j > SUBLANES else SUBLANES
            s_piece = s0[i] + s1_all[0:rows]
            i_piece = i0[i] * PEER_NKEYS + i1_all[0:rows]
            if n_j < rows:
                keep = lax.broadcasted_iota(I32, s_piece.shape, 0) < n_j
                s_piece = jnp.where(keep, s_piece, -jnp.inf)
            cand_s.append(s_piece)
            cand_i.append(i_piece)
        best_s, best_i = _extract_top(jnp.concatenate(cand_s, axis=0),
                                      jnp.concatenate(cand_i, axis=0), PEER_TOPK)
        e = [jnp.exp(s - best_s[0]) for s in best_s]
        den = e[0]
        for t in e[1:]:
            den = den + t
        idx_rows += best_i
        gate_rows += [t / den for t in e]
    idx_ref[...] = jnp.concatenate(idx_rows, axis=0)
    gate_ref[...] = jnp.concatenate(gate_rows, axis=0)


def _peer_route(x2, g, scale, shift, wq, sk, seq, tm=256):
    t, d = x2.shape
    nq = wq.shape[1]
    nsel = PEER_HEADS * PEER_TOPK
    per_batch = lambda i: ((i * tm) // seq, 0, 0)
    return pl.pallas_call(
        _peer_route_kernel,
        grid=(t // tm,),
        in_specs=[
            pl.BlockSpec((tm, d), lambda i: (i, 0)),
            pl.BlockSpec((1, d), lambda i: (0, 0)),
            pl.BlockSpec((1, 1, d), per_batch),
            pl.BlockSpec((1, 1, d), per_batch),
            _resident((d, nq), lambda i: (0, 0)),
            _resident(sk.shape, lambda i: (0, 0, 0)),
        ],
        out_specs=[
            pl.BlockSpec((tm, d), lambda i: (i, 0)),
            pl.BlockSpec((nsel, tm), lambda i: (0, i)),
            pl.BlockSpec((nsel, tm), lambda i: (0, i)),
        ],
        out_shape=[
            jax.ShapeDtypeStruct((t, d), F32),
            jax.ShapeDtypeStruct((nsel, t), I32),
            jax.ShapeDtypeStruct((nsel, t), F32),
        ],
        compiler_params=_params("arbitrary"),
        name="peer_route",
    )(x2, g.reshape(1, d), scale, shift, wq, sk)


def _reduce_token(tok, j, rows, gate_ref, h_ref, x_ref, mod_ref, o_ref):
    nsel = PEER_HEADS * PEER_TOPK
    h_row = h_ref[pl.ds(tok, 1), :]
    words = rows[j * nsel:(j + 1) * nsel, :]
    u_rows = lax.bitcast_convert_type(words & jnp.int32(-65536), F32)
    act = jnp.sum(u_rows * h_row, axis=-1, keepdims=True)
    gate_col = pltpu.roll(gate_ref[...], (PEER_TOK - tok) & (PEER_TOK - 1), 1)[:, 0:1]
    w = gate_col * (0.5 * act * (1.0 + lax.erf(act * np.float32(np.sqrt(0.5)))))
    v_rows = lax.bitcast_convert_type(words << 16, F32)
    out_row = jnp.sum(v_rows * w, axis=0, keepdims=True)
    o_ref[pl.ds(tok, 1), :] = x_ref[pl.ds(tok, 1), :] + mod_ref[0] * out_row


def _peer_expert_kernel(idx_ref, idx_next_ref, gate_ref, h_ref, x_ref, mod_ref, tab_ref, o_ref,
                        rows_a, rows_b, sem):
    i = pl.program_id(0)
    nsel = PEER_HEADS * PEER_TOPK
    n_sub = PEER_TOK // PEER_SUB
    bufs = ((rows_a, sem.at[0]), (rows_b, sem.at[1]))

    def row_copy(src_idx_ref, flat, dst, dst_sem, r):
        e = src_idx_ref[flat]
        return pltpu.make_async_copy(tab_ref.at[e], dst.at[pl.ds(r, 1)], dst_sem)

    def start_token(src_idx_ref, base, j, dst, dst_sem):
        for kq in range(nsel):
            row_copy(src_idx_ref, base + j * nsel + kq, dst, dst_sem, j * nsel + kq).start(priority=kq % 2)

    def wait_batch(dst, dst_sem):
        pltpu.make_async_copy(tab_ref.at[pl.ds(0, PEER_SUB * nsel), 0], dst, dst_sem).wait()

    def batch(s, parity, src_idx_ref, next_base):
        src, src_sem = bufs[parity]
        dst, dst_sem = bufs[1 - parity]
        wait_batch(src, src_sem)
        for j in range(PEER_SUB):
            start_token(src_idx_ref, next_base, j, dst, dst_sem)
            _reduce_token(s * PEER_SUB + j, j, src, gate_ref, h_ref, x_ref, mod_ref, o_ref)

    @pl.when(i == 0)
    def _():
        def first(r, _):
            row_copy(idx_ref, r, rows_a, sem.at[0], r).start()
            return 0
        lax.fori_loop(0, PEER_SUB * nsel, first, 0)

    def pair(p, _):
        s = 2 * p
        batch(s, 0, idx_ref, (s + 1) * PEER_SUB * nsel)
        batch(s + 1, 1, idx_ref, (s + 2) * PEER_SUB * nsel)
        return 0

    lax.fori_loop(0, n_sub // 2 - 1, pair, 0)
    batch(n_sub - 2, 0, idx_ref, (n_sub - 1) * PEER_SUB * nsel)
    batch(n_sub - 1, 1, idx_next_ref, 0)

    @pl.when(i == pl.num_programs(0) - 1)
    def _():
        wait_batch(rows_a, sem.at[0])


def _peer_experts(idx_flat, gate_t, h2, x2, gate, table3, seq, n_tok):
    d = x2.shape[1]
    nsel = PEER_HEADS * PEER_TOPK
    n_steps = n_tok // PEER_TOK
    per_batch = lambda i: ((i * PEER_TOK) // seq, 0, 0)
    idx_block = PEER_TOK * nsel
    return pl.pallas_call(
        _peer_expert_kernel,
        grid=(n_steps,),
        in_specs=[
            pl.BlockSpec((idx_block,), lambda i: (i,), memory_space=pltpu.SMEM),
            pl.BlockSpec((idx_block,), lambda i: (jnp.minimum(i + 1, n_steps - 1),),
                         memory_space=pltpu.SMEM),
            pl.BlockSpec((nsel, PEER_TOK), lambda i: (0, i)),
            pl.BlockSpec((PEER_TOK, d), lambda i: (i, 0)),
            pl.BlockSpec((PEER_TOK, d), lambda i: (i, 0)),
            pl.BlockSpec((1, 1, d), per_batch),
            pl.BlockSpec(memory_space=pl.ANY),
        ],
        out_specs=pl.BlockSpec((PEER_TOK, d), lambda i: (i, 0)),
        out_shape=jax.ShapeDtypeStruct((n_tok, d), F32),
        scratch_shapes=[
            pltpu.VMEM((PEER_SUB * nsel, d), I32),
            pltpu.VMEM((PEER_SUB * nsel, d), I32),
            pltpu.SemaphoreType.DMA((2,)),
        ],
        compiler_params=_params("arbitrary"),
        name="peer_experts",
    )(idx_flat, idx_flat, gate_t, h2, x2, gate, table3)


def _sc_gather_rows(table, idx_flat, row_start, n_rows):
    d = table.shape[1]
    n_workers = SC_CORES * SC_SUBCORES
    assert n_rows % (n_workers * SC_WINDOW) == 0
    per_worker = n_rows // n_workers
    mesh = plsc.VectorSubcoreMesh(core_axis_name="c", subcore_axis_name="s")

    @functools.partial(
        pl.kernel, mesh=mesh,
        out_type=jax.ShapeDtypeStruct((n_rows, d), table.dtype),
        scratch_types=[
            pltpu.VMEM((SC_WINDOW,), I32),
            pltpu.VMEM((SC_WINDOW, d), table.dtype),
            pltpu.SemaphoreType.DMA,
        ],
        name="sc_gather_rows",
    )
    def gather(table_hbm, idx_hbm, out_hbm, idx_v, rows_v, sem):
        base = (lax.axis_index("s") * SC_CORES + lax.axis_index("c")) * per_worker

        @pl.loop(0, per_worker // SC_WINDOW)
        def _(w):
            off = base + w * SC_WINDOW
            pltpu.sync_copy(idx_hbm.at[pl.ds(row_start + off, SC_WINDOW)], idx_v)
            pltpu.async_copy(table_hbm.at[idx_v], rows_v, sem).wait()
            pltpu.sync_copy(rows_v, out_hbm.at[pl.ds(off, SC_WINDOW)])

    return gather(table, idx_flat)


def _peer_staged_kernel(gate_ref, h_ref, x_ref, mod_ref, rows_ref, o_ref):
    part = pl.program_id(1)
    for j in range(PEER_STAGE_TOK):
        _reduce_token(part * PEER_STAGE_TOK + j, j, rows_ref, gate_ref, h_ref, x_ref, mod_ref, o_ref)


def _peer_experts_staged(rows, gate_t, h2, x2, gate, seq, tok_start, n_tok):
    d = x2.shape[1]
    nsel = PEER_HEADS * PEER_TOPK
    first = tok_start // PEER_TOK
    parts = PEER_TOK // PEER_STAGE_TOK
    return pl.pallas_call(
        _peer_staged_kernel,
        grid=(n_tok // PEER_TOK, parts),
        in_specs=[
            pl.BlockSpec((nsel, PEER_TOK), lambda i, p: (0, first + i)),
            pl.BlockSpec((PEER_TOK, d), lambda i, p: (first + i, 0)),
            pl.BlockSpec((PEER_TOK, d), lambda i, p: (first + i, 0)),
            pl.BlockSpec((1, 1, d), lambda i, p: (((first + i) * PEER_TOK) // seq, 0, 0)),
            pl.BlockSpec((PEER_STAGE_TOK * nsel, d), lambda i, p: (i * parts + p, 0)),
        ],
        out_specs=pl.BlockSpec((PEER_TOK, d), lambda i, p: (i, 0)),
        out_shape=jax.ShapeDtypeStruct((n_tok, d), F32),
        compiler_params=_params("arbitrary", "arbitrary"),
        name="peer_experts_staged",
    )(gate_t, h2, x2, gate, rows)


def _final_norm_kernel(x_ref, g_ref, o_ref):
    x = x_ref[...]
    ms = jnp.mean(x * x, axis=-1, keepdims=True)
    o_ref[...] = x * lax.rsqrt(ms + NORM_EPS) * g_ref[...]


def _final_norm(x2, g, tm=512):
    t, d = x2.shape
    return pl.pallas_call(
        _final_norm_kernel,
        grid=(t // tm,),
        in_specs=[pl.BlockSpec((tm, d), lambda i: (i, 0)), pl.BlockSpec((1, d), lambda i: (0, 0))],
        out_specs=pl.BlockSpec((tm, d), lambda i: (i, 0)),
        out_shape=jax.ShapeDtypeStruct((t, d), F32),
        compiler_params=_params("arbitrary"),
        name="final_norm",
    )(x2, g.reshape(1, d))


def _pack_expert_rows(u, v):
    hi = lax.bitcast_convert_type(u.astype(BF16), jnp.uint16).astype(jnp.uint32)
    lo = lax.bitcast_convert_type(v.astype(BF16), jnp.uint16).astype(jnp.uint32)
    return lax.bitcast_convert_type((hi << 16) | lo, I32)


def _reorder_in_cols(w):
    n_qkv = SWA_WIDTH + 2 * SWA_KV_WIDTH + 3 * SB_WIDTH
    return jnp.concatenate([w[:, n_qkv:], w[:, :n_qkv]], axis=1)


def kernel(x, c, positions, ada_w, ada_b, norm1_g, w_in, swa_sinks, w_branch_swa, w_branch_sb,
           w_out, norm2_g, peer_wq, peer_subkeys, peer_u, peer_v, final_g):
    b, s, d = x.shape
    depth = ada_w.shape[0]
    t = b * s
    mod = _ada_mod(c, ada_w, ada_b)
    pos3 = positions.reshape(b, s, 1)
    inv_freq = jnp.power(ROPE_THETA, -jnp.arange(HEAD_DIM // 2, dtype=F32) * (2.0 / HEAD_DIM))
    invf = jnp.tile(inv_freq, LANES // (HEAD_DIM // 2)).reshape(1, LANES)
    x2 = x.reshape(t, d)
    n_blocks = t // PEER_TOK
    tc_blocks = max(1, int(n_blocks * PEER_TC_SHARE))
    bounds = [tc_blocks + ((n_blocks - tc_blocks) * k) // PEER_SC_CHUNKS for k in range(PEER_SC_CHUNKS + 1)]
    for l in range(depth):
        m = mod[l].reshape(b, N_MOD, 1, d)
        shift1, scale1, gate1, shift2, scale2, gate2 = [m[:, i] for i in range(N_MOD)]
        w_in_l = _reorder_in_cols(w_in[l]).astype(BF16)
        proj = _norm_proj(x2, norm1_g[l], scale1, shift1, w_in_l, s)
        proj3 = proj.reshape(b, s, IN_COLS)
        y_a = _swa(proj3, pos3, invf, swa_sinks[l])
        y_b = _stick_breaking(proj3)
        x2 = _merge(y_a.reshape(t, SWA_WIDTH), y_b.reshape(t, SB_WIDTH), proj, x2, gate1,
                    w_branch_swa[l].astype(BF16), w_branch_sb[l].astype(BF16),
                    w_out[l].astype(BF16), s)
        sk = peer_subkeys[l].reshape(PEER_HEADS * 2, PEER_NKEYS, PEER_HALF).astype(BF16)
        h2, idx_t, gate_t = _peer_route(x2, norm2_g[l], scale2, shift2, peer_wq[l].astype(BF16), sk, s)
        table = _pack_expert_rows(peer_u[l], peer_v[l])
        idx_flat = idx_t.T.reshape(-1)
        nsel = PEER_HEADS * PEER_TOPK
        parts = [_peer_experts(idx_flat, gate_t, h2, x2, gate2, table[:, None, :], s,
                               bounds[0] * PEER_TOK)]
        for lo, hi in zip(bounds[:-1], bounds[1:]):
            start, n_tok = lo * PEER_TOK, (hi - lo) * PEER_TOK
            staged = _sc_gather_rows(table, idx_flat, start * nsel, n_tok * nsel)
            parts.append(_peer_experts_staged(staged, gate_t, h2, x2, gate2, s, start, n_tok))
        x2 = jnp.concatenate(parts, axis=0)
    return _final_norm(x2, final_g).reshape(b, s, d)
```

```python
import functools

import jax
import jax.numpy as jnp
import numpy as np
from jax import lax
from jax.experimental import pallas as pl
from jax.experimental.pallas import tpu as pltpu
from jax.experimental.pallas import tpu_sc as plsc

F32 = jnp.float32
BF16 = jnp.bfloat16
I32 = jnp.int32

HEAD_DIM = 64
BLOCK = 128
SWA_Q_HEADS = 8
SWA_KV_HEADS = 2
SB_HEADS = 8
ROPE_THETA = 10000.0
SWA_WIDTH = SWA_Q_HEADS * HEAD_DIM
SWA_KV_WIDTH = SWA_KV_HEADS * HEAD_DIM
SB_WIDTH = SB_HEADS * HEAD_DIM
PEER_HEADS = 8
PEER_NKEYS = 128
PEER_TOPK = 16
PEER_QDIM = 256
PEER_HALF = PEER_QDIM // 2
N_MOD = 6
NORM_EPS = 1e-6
NEG_INF = -1e30

LANES = 128
SUBLANES = 8
VMEM_LIMIT_BYTES = 56 * 1024 * 1024

COL_GATE_SWA = 0
COL_GATE_SB = 1024
COL_Q_SWA = 2048
COL_K_SWA = COL_Q_SWA + SWA_WIDTH
COL_V_SWA = COL_K_SWA + SWA_KV_WIDTH
COL_Q_SB = COL_V_SWA + SWA_KV_WIDTH
COL_K_SB = COL_Q_SB + SB_WIDTH
COL_V_SB = COL_K_SB + SB_WIDTH
IN_COLS = COL_V_SB + SB_WIDTH

SB_CHUNK = 2 * BLOCK
PEER_SUB = 4
PEER_TOK = 128
PEER_STAGE_TOK = 8
PEER_TC_SHARE = 0.3
PEER_SC_CHUNKS = 3

SC_CORES = 2
SC_SUBCORES = 16
SC_WINDOW = 64


def _dot(a, b):
    return jnp.dot(a, b, preferred_element_type=F32)


def _dot_nt(a, b):
    return lax.dot_general(a, b, (((1,), (1,)), ((), ())), preferred_element_type=F32)


def _split_bf16(a):
    hi = a.astype(BF16)
    lo = (a - hi.astype(F32)).astype(BF16)
    return hi, lo


def _params(*sem):
    return pltpu.CompilerParams(dimension_semantics=sem, vmem_limit_bytes=VMEM_LIMIT_BYTES)


def _resident(shape, index_map):
    return pl.BlockSpec(shape, index_map, pipeline_mode=pl.Buffered(1))


def _ada_kernel(c_ref, w_ref, b_ref, o_ref):
    c = c_ref[...]
    a = c * jax.nn.sigmoid(c)
    a_hi, a_lo = _split_bf16(a)
    w_hi, w_lo = _split_bf16(w_ref[0])
    acc = _dot(a_hi, w_hi) + _dot(a_hi, w_lo) + _dot(a_lo, w_hi)
    o_ref[0] = acc + b_ref[0]


def _ada_mod(c, ada_w, ada_b):
    depth, d, n = ada_w.shape
    b = c.shape[0]
    tn = n // 4
    return pl.pallas_call(
        _ada_kernel,
        grid=(depth, n // tn),
        in_specs=[
            pl.BlockSpec((b, d), lambda l, j: (0, 0)),
            pl.BlockSpec((1, d, tn), lambda l, j: (l, 0, j)),
            pl.BlockSpec((1, 1, tn), lambda l, j: (l, 0, j)),
        ],
        out_specs=pl.BlockSpec((1, b, tn), lambda l, j: (l, 0, j)),
        out_shape=jax.ShapeDtypeStruct((depth, b, n), F32),
        compiler_params=_params("arbitrary", "arbitrary"),
        name="ada_mod",
    )(c, ada_w, ada_b.reshape(depth, 1, n))


def _modulated_norm(x, g, scale, shift):
    ms = jnp.mean(x * x, axis=-1, keepdims=True)
    y = x * lax.rsqrt(ms + NORM_EPS) * g
    return y * (1.0 + scale) + shift


def _norm_proj_kernel(x_ref, g_ref, scale_ref, shift_ref, w_ref, o_ref, *, col_chunk):
    h = _modulated_norm(x_ref[...], g_ref[...], scale_ref[0], shift_ref[0]).astype(BF16)
    n = o_ref.shape[1]
    for c0 in range(0, n, col_chunk):
        c1 = min(c0 + col_chunk, n)
        o_ref[:, c0:c1] = _dot(h, w_ref[:, c0:c1]).astype(o_ref.dtype)


def _norm_proj(x2, g, scale, shift, w_bf16, seq, tm=512):
    t, d = x2.shape
    n = w_bf16.shape[1]
    per_batch = lambda i: ((i * tm) // seq, 0, 0)
    return pl.pallas_call(
        functools.partial(_norm_proj_kernel, col_chunk=1024),
        grid=(t // tm,),
        in_specs=[
            pl.BlockSpec((tm, d), lambda i: (i, 0)),
            pl.BlockSpec((1, d), lambda i: (0, 0)),
            pl.BlockSpec((1, 1, d), per_batch),
            pl.BlockSpec((1, 1, d), per_batch),
            _resident((d, n), lambda i: (0, 0)),
        ],
        out_specs=pl.BlockSpec((tm, n), lambda i: (i, 0)),
        out_shape=jax.ShapeDtypeStruct((t, n), BF16),
        compiler_params=_params("arbitrary"),
        name="norm_proj",
    )(x2, g.reshape(1, d), scale, shift, w_bf16)


def _swa_kernel(sink_ref, q_ref, kc_ref, kp_ref, vc_ref, vp_ref, pc_ref, pp_ref, invf_ref, o_ref):
    n = pl.program_id(1)
    lane = lax.broadcasted_iota(I32, (BLOCK, LANES), 1)
    first_half = (lane % HEAD_DIM) < (HEAD_DIM // 2)
    low_head = lane < HEAD_DIM
    invf = invf_ref[...]

    def rope_tables(pos_ref):
        ang = pos_ref[0].astype(F32) * invf
        s = jnp.sin(ang)
        return jnp.cos(ang), jnp.where(first_half, -s, s)

    def rope(x, tables):
        c, s_signed = tables
        swapped = jnp.where(first_half, pltpu.roll(x, LANES - HEAD_DIM // 2, 1),
                            pltpu.roll(x, HEAD_DIM // 2, 1))
        return x * c + swapped * s_signed

    tab_c = rope_tables(pc_ref)
    tab_p = rope_tables(pp_ref)
    k = jnp.concatenate([rope(kp_ref[0].astype(F32), tab_p), rope(kc_ref[0].astype(F32), tab_c)], axis=0)
    v = jnp.concatenate([vp_ref[0], vc_ref[0]], axis=0).astype(F32)
    k_var = (k.astype(BF16), pltpu.roll(k, HEAD_DIM, 1).astype(BF16))
    v_var = (v.astype(BF16), pltpu.roll(v, HEAD_DIM, 1).astype(BF16))

    row = lax.broadcasted_iota(I32, (BLOCK, 2 * BLOCK), 0)
    col = lax.broadcasted_iota(I32, (BLOCK, 2 * BLOCK), 1)
    diff = row + BLOCK - col
    valid = (diff >= 0) & (diff < BLOCK) & ((col >= BLOCK) | (n > 0))

    group = SWA_Q_HEADS // SWA_KV_HEADS
    for jt in range(SWA_Q_HEADS // 2):
        q_t = rope(q_ref[0, :, jt * LANES:(jt + 1) * LANES].astype(F32), tab_c) * (HEAD_DIM ** -0.5)
        outs = []
        for hh in range(2):
            j = 2 * jt + hh
            g = j // group
            qm = jnp.where(low_head == (hh == 0), q_t, 0.0).astype(BF16)
            s = _dot_nt(qm, k_var[0 if g == hh else 1])
            s = jnp.where(valid, s, NEG_INF)
            sink = sink_ref[j]
            m = jnp.maximum(jnp.max(s, axis=-1, keepdims=True), sink)
            p = jnp.exp(s - m)
            den = jnp.sum(p, axis=-1, keepdims=True) + jnp.exp(sink - m)
            outs.append(_dot(p.astype(BF16), v_var[0 if g == hh else 1]) / den)
        o_ref[0, :, jt * LANES:(jt + 1) * LANES] = jnp.where(low_head, outs[0], outs[1]).astype(o_ref.dtype)


def _swa(proj3, pos3, invf, sinks):
    b, s, _ = proj3.shape
    nb = s // BLOCK
    cur = lambda c: (lambda i, n: (i, n, c))
    prev = lambda c: (lambda i, n: (i, jnp.maximum(n - 1, 0), c))
    return pl.pallas_call(
        _swa_kernel,
        grid=(b, nb),
        in_specs=[
            pl.BlockSpec(memory_space=pltpu.SMEM),
            pl.BlockSpec((1, BLOCK, SWA_WIDTH), cur(COL_Q_SWA // SWA_WIDTH)),
            pl.BlockSpec((1, BLOCK, LANES), cur(COL_K_SWA // LANES)),
            pl.BlockSpec((1, BLOCK, LANES), prev(COL_K_SWA // LANES)),
            pl.BlockSpec((1, BLOCK, LANES), cur(COL_V_SWA // LANES)),
            pl.BlockSpec((1, BLOCK, LANES), prev(COL_V_SWA // LANES)),
            pl.BlockSpec((1, BLOCK, 1), cur(0)),
            pl.BlockSpec((1, BLOCK, 1), prev(0)),
            pl.BlockSpec((1, LANES), lambda i, n: (0, 0)),
        ],
        out_specs=pl.BlockSpec((1, BLOCK, SWA_WIDTH), lambda i, n: (i, n, 0)),
        out_shape=jax.ShapeDtypeStruct((b, s, SWA_WIDTH), BF16),
        compiler_params=_params("arbitrary", "arbitrary"),
        name="swa",
    )(sinks, proj3, proj3, proj3, proj3, proj3, pos3, pos3, invf)


def _sb_kernel(q_ref, k_ref, v_ref, tri_ref, o_ref):
    qb = pl.program_id(2)
    lane = lax.broadcasted_iota(I32, (SB_CHUNK, LANES), 1)
    low_head = lane < HEAD_DIM
    q = q_ref[0].astype(F32) * (HEAD_DIM ** -0.5)
    qms = [jnp.where(low_head == (h == 0), q, 0.0).astype(BF16) for h in range(2)]
    tri = tri_ref[...]
    row = lax.broadcasted_iota(I32, (SB_CHUNK, SB_CHUNK), 0)
    col = lax.broadcasted_iota(I32, (SB_CHUNK, SB_CHUNK), 1)
    strict = col < row

    def sweep_pair(c0, carry, diagonal):
        c1 = c0 - 1
        exists = c1 >= 0
        off0 = pl.multiple_of(c0 * SB_CHUNK, SB_CHUNK)
        off1 = pl.multiple_of(jnp.maximum(c1, 0) * SB_CHUNK, SB_CHUNK)
        k0 = k_ref[0, pl.ds(off0, SB_CHUNK), :]
        v0 = v_ref[0, pl.ds(off0, SB_CHUNK), :]
        k1 = k_ref[0, pl.ds(off1, SB_CHUNK), :]
        v1 = v_ref[0, pl.ds(off1, SB_CHUNK), :]
        v1 = jnp.where(exists, v1, jnp.zeros_like(v1))
        chains = [(h, kc, vc, diagonal and first) for h in range(2)
                  for kc, vc, first in ((k0, v0, True), (k1, v1, False))]
        zs = [_dot_nt(qms[h], kc) for h, kc, _, _ in chains]
        log_1m = []
        for z, (_, _, _, masked) in zip(zs, chains):
            neg = -z
            t = jnp.minimum(neg, 0.0) - jnp.log(1.0 + jnp.exp(jnp.minimum(z, neg)))
            log_1m.append(jnp.where(strict, t, 0.0) if masked else t)
        sums = [jnp.sum(t, axis=-1, keepdims=True) for t in log_1m]
        after = [_dot(t.astype(BF16), tri) for t in log_1m]
        new = []
        for h in range(2):
            acc, later = carry[2 * h], carry[2 * h + 1]
            for n, later_n in ((2 * h, later), (2 * h + 1, later + sums[2 * h])):
                attn = jnp.exp(zs[n] + log_1m[n] + after[n] + later_n)
                if chains[n][3]:
                    attn = jnp.where(strict, attn, 0.0)
                acc = acc + _dot(attn.astype(BF16), chains[n][2])
            new.append(acc)
            new.append(later + sums[2 * h] + sums[2 * h + 1])
        return tuple(new)

    zeros = (jnp.zeros((SB_CHUNK, LANES), F32), jnp.zeros((SB_CHUNK, 1), F32))
    carry = sweep_pair(qb, zeros + zeros, True)
    n_pairs = lax.shift_right_logical(qb + 2, 1)
    carry = lax.fori_loop(1, n_pairs, lambda i, c: sweep_pair(qb - 2 * i, c, False), carry)
    o_ref[0] = jnp.where(low_head, carry[0], carry[2]).astype(o_ref.dtype)


def _stick_breaking(proj3):
    b, s, _ = proj3.shape
    assert s % SB_CHUNK == 0
    nb = s // SB_CHUNK
    pairs = SB_HEADS // 2
    tri = (np.arange(SB_CHUNK)[:, None] > np.arange(SB_CHUNK)[None, :])
    tri = jnp.asarray(tri, dtype=BF16)
    return pl.pallas_call(
        _sb_kernel,
        grid=(b, pairs, nb),
        in_specs=[
            pl.BlockSpec((1, SB_CHUNK, LANES), lambda i, p, n: (i, n, COL_Q_SB // LANES + p)),
            pl.BlockSpec((1, s, LANES), lambda i, p, n: (i, 0, COL_K_SB // LANES + p)),
            pl.BlockSpec((1, s, LANES), lambda i, p, n: (i, 0, COL_V_SB // LANES + p)),
            pl.BlockSpec((SB_CHUNK, SB_CHUNK), lambda i, p, n: (0, 0)),
        ],
        out_specs=pl.BlockSpec((1, SB_CHUNK, LANES), lambda i, p, n: (i, n, p)),
        out_shape=jax.ShapeDtypeStruct((b, s, SB_WIDTH), BF16),
        compiler_params=_params("arbitrary", "arbitrary", "arbitrary"),
        name="stick_breaking",
    )(proj3, proj3, proj3, tri)


def _merge_kernel(ya_ref, yb_ref, ga_ref, gb_ref, x_ref, gate_ref, wa_ref, wb_ref, wo_ref, o_ref):
    ma = _dot(ya_ref[...], wa_ref[...])
    mb = _dot(yb_ref[...], wb_ref[...])
    merged = (jax.nn.sigmoid(ga_ref[...].astype(F32)) * ma
              + jax.nn.sigmoid(gb_ref[...].astype(F32)) * mb)
    o_ref[...] = x_ref[...] + gate_ref[0] * _dot(merged.astype(BF16), wo_ref[...])


def _merge(ya, yb, proj, x2, gate, wa, wb, wo, seq, tm=512):
    t, d = x2.shape
    per_batch = lambda i: ((i * tm) // seq, 0, 0)
    return pl.pallas_call(
        _merge_kernel,
        grid=(t // tm,),
        in_specs=[
            pl.BlockSpec((tm, SWA_WIDTH), lambda i: (i, 0)),
            pl.BlockSpec((tm, SB_WIDTH), lambda i: (i, 0)),
            pl.BlockSpec((tm, d), lambda i: (i, COL_GATE_SWA // d)),
            pl.BlockSpec((tm, d), lambda i: (i, COL_GATE_SB // d)),
            pl.BlockSpec((tm, d), lambda i: (i, 0)),
            pl.BlockSpec((1, 1, d), per_batch),
            _resident((SWA_WIDTH, d), lambda i: (0, 0)),
            _resident((SB_WIDTH, d), lambda i: (0, 0)),
            _resident((d, d), lambda i: (0, 0)),
        ],
        out_specs=pl.BlockSpec((tm, d), lambda i: (i, 0)),
        out_shape=jax.ShapeDtypeStruct((t, d), F32),
        compiler_params=_params("arbitrary"),
        name="merge_out",
    )(ya, yb, proj, proj, x2, gate, wa, wb, wo)


def _extract_top(vals, payload, k):
    r = vals.shape[0]
    idx = lax.broadcasted_iota(I32, vals.shape, 0)
    top_v, top_p = [], []
    for _ in range(k):
        m = jnp.max(vals, axis=0, keepdims=True)
        first = jnp.min(jnp.where(vals == m, idx, r), axis=0, keepdims=True)
        hit = idx == first
        top_v.append(m)
        if payload is None:
            top_p.append(first)
        else:
            top_p.append(jnp.sum(jnp.where(hit, payload, 0), axis=0, keepdims=True))
        vals = jnp.where(hit, -jnp.inf, vals)
    return top_v, top_p


def _peer_route_kernel(x_ref, g_ref, scale_ref, shift_ref, wq_ref, sk_ref, h_ref, idx_ref, gate_ref):
    h = _modulated_norm(x_ref[...], g_ref[...], scale_ref[0], shift_ref[0])
    h_ref[...] = h
    q = _dot(h.astype(BF16), wq_ref[...]).astype(BF16)
    idx_rows, gate_rows = [], []
    for head in range(PEER_HEADS):
        halves = []
        for p in range(2):
            c0 = (head * 2 + p) * PEER_HALF
            scores = _dot_nt(sk_ref[head * 2 + p], q[:, c0:c0 + PEER_HALF])
            halves.append(_extract_top(scores, None, PEER_TOPK))
        (s0, i0), (s1, i1) = halves
        s1_all = jnp.concatenate(s1, axis=0)
        i1_all = jnp.concatenate(i1, axis=0)
        cand_s, cand_i = [], []
        for i in range(PEER_TOPK):
            n_j = PEER_TOPK // (i + 1)
            rows = PEER_TOPK if n_j > SUBLANES else SUBLANES
            s_piece = s0[i] + s1_all[0:rows]
            i_piece = i0[i] * PEER_NKEYS + i1_all[0:rows]
            if n_j < rows:
                keep = lax.broadcasted_iota(I32, s_piece.shape, 0) < n_j
                s_piece = jnp.where(keep, s_piece, -jnp.inf)
            cand_s.append(s_piece)
            cand_i.append(i_piece)
        best_s, best_i = _extract_top(jnp.concatenate(cand_s, axis=0),
                                      jnp.concatenate(cand_i, axis=0), PEER_TOPK)
        e = [jnp.exp(s - best_s[0]) for s in best_s]
        den = e[0]
        for t in e[1:]:
            den = den + t
        idx_rows += best_i
        gate_rows += [t / den for t in e]
    idx_ref[...] = jnp.concatenate(idx_rows, axis=0)
    gate_ref[...] = jnp.concatenate(gate_rows, axis=0)


def _peer_route(x2, g, scale, shift, wq, sk, seq, tm=256):
    t, d = x2.shape
    nq = wq.shape[1]
    nsel = PEER_HEADS * PEER_TOPK
    per_batch = lambda i: ((i * tm) // seq, 0, 0)
    return pl.pallas_call(
        _peer_route_kernel,
        grid=(t // tm,),
        in_specs=[
            pl.BlockSpec((tm, d), lambda i: (i, 0)),
            pl.BlockSpec((1, d), lambda i: (0, 0)),
            pl.BlockSpec((1, 1, d), per_batch),
            pl.BlockSpec((1, 1, d), per_batch),
            _resident((d, nq), lambda i: (0, 0)),
            _resident(sk.shape, lambda i: (0, 0, 0)),
        ],
        out_specs=[
            pl.BlockSpec((tm, d), lambda i: (i, 0)),
            pl.BlockSpec((nsel, tm), lambda i: (0, i)),
            pl.BlockSpec((nsel, tm), lambda i: (0, i)),
        ],
        out_shape=[
            jax.ShapeDtypeStruct((t, d), F32),
            jax.ShapeDtypeStruct((nsel, t), I32),
            jax.ShapeDtypeStruct((nsel, t), F32),
        ],
        compiler_params=_params("arbitrary"),
        name="peer_route",
    )(x2, g.reshape(1, d), scale, shift, wq, sk)


def _reduce_token(tok, j, rows, gate_ref, h_ref, x_ref, mod_ref, o_ref):
    nsel = PEER_HEADS * PEER_TOPK
    h_row = h_ref[pl.ds(tok, 1), :]
    words = rows[j * nsel:(j + 1) * nsel, :]
    u_rows = lax.bitcast_convert_type(words & jnp.int32(-65536), F32)
    act = jnp.sum(u_rows * h_row, axis=-1, keepdims=True)
    gate_col = pltpu.roll(gate_ref[...], (PEER_TOK - tok) & (PEER_TOK - 1), 1)[:, 0:1]
    w = gate_col * (0.5 * act * (1.0 + lax.erf(act * np.float32(np.sqrt(0.5)))))
    v_rows = lax.bitcast_convert_type(words << 16, F32)
    out_row = jnp.sum(v_rows * w, axis=0, keepdims=True)
    o_ref[pl.ds(tok, 1), :] = x_ref[pl.ds(tok, 1), :] + mod_ref[0] * out_row


def _peer_expert_kernel(idx_ref, idx_next_ref, gate_ref, h_ref, x_ref, mod_ref, tab_ref, o_ref,
                        rows_a, rows_b, sem):
    i = pl.program_id(0)
    nsel = PEER_HEADS * PEER_TOPK
    n_sub = PEER_TOK // PEER_SUB
    bufs = ((rows_a, sem.at[0]), (rows_b, sem.at[1]))

    def row_copy(src_idx_ref, flat, dst, dst_sem, r):
        e = src_idx_ref[flat]
        return pltpu.make_async_copy(tab_ref.at[e], dst.at[pl.ds(r, 1)], dst_sem)

    def start_token(src_idx_ref, base, j, dst, dst_sem):
        for kq in range(nsel):
            row_copy(src_idx_ref, base + j * nsel + kq, dst, dst_sem, j * nsel + kq).start(priority=kq % 2)

    def wait_batch(dst, dst_sem):
        pltpu.make_async_copy(tab_ref.at[pl.ds(0, PEER_SUB * nsel), 0], dst, dst_sem).wait()

    def batch(s, parity, src_idx_ref, next_base):
        src, src_sem = bufs[parity]
        dst, dst_sem = bufs[1 - parity]
        wait_batch(src, src_sem)
        for j in range(PEER_SUB):
            start_token(src_idx_ref, next_base, j, dst, dst_sem)
            _reduce_token(s * PEER_SUB + j, j, src, gate_ref, h_ref, x_ref, mod_ref, o_ref)

    @pl.when(i == 0)
    def _():
        def first(r, _):
            row_copy(idx_ref, r, rows_a, sem.at[0], r).start()
            return 0
        lax.fori_loop(0, PEER_SUB * nsel, first, 0)

    def pair(p, _):
        s = 2 * p
        batch(s, 0, idx_ref, (s + 1) * PEER_SUB * nsel)
        batch(s + 1, 1, idx_ref, (s + 2) * PEER_SUB * nsel)
        return 0

    lax.fori_loop(0, n_sub // 2 - 1, pair, 0)
    batch(n_sub - 2, 0, idx_ref, (n_sub - 1) * PEER_SUB * nsel)
    batch(n_sub - 1, 1, idx_next_ref, 0)

    @pl.when(i == pl.num_programs(0) - 1)
    def _():
        wait_batch(rows_a, sem.at[0])


def _peer_experts(idx_flat, gate_t, h2, x2, gate, table3, seq, n_tok):
    d = x2.shape[1]
    nsel = PEER_HEADS * PEER_TOPK
    n_steps = n_tok // PEER_TOK
    per_batch = lambda i: ((i * PEER_TOK) // seq, 0, 0)
    idx_block = PEER_TOK * nsel
    return pl.pallas_call(
        _peer_expert_kernel,
        grid=(n_steps,),
        in_specs=[
            pl.BlockSpec((idx_block,), lambda i: (i,), memory_space=pltpu.SMEM),
            pl.BlockSpec((idx_block,), lambda i: (jnp.minimum(i + 1, n_steps - 1),),
                         memory_space=pltpu.SMEM),
            pl.BlockSpec((nsel, PEER_TOK), lambda i: (0, i)),
            pl.BlockSpec((PEER_TOK, d), lambda i: (i, 0)),
            pl.BlockSpec((PEER_TOK, d), lambda i: (i, 0)),
            pl.BlockSpec((1, 1, d), per_batch),
            pl.BlockSpec(memory_space=pl.ANY),
        ],
        out_specs=pl.BlockSpec((PEER_TOK, d), lambda i: (i, 0)),
        out_shape=jax.ShapeDtypeStruct((n_tok, d), F32),
        scratch_shapes=[
            pltpu.VMEM((PEER_SUB * nsel, d), I32),
            pltpu.VMEM((PEER_SUB * nsel, d), I32),
            pltpu.SemaphoreType.DMA((2,)),
        ],
        compiler_params=_params("arbitrary"),
        name="peer_experts",
    )(idx_flat, idx_flat, gate_t, h2, x2, gate, table3)


def _sc_gather_rows(table, idx_flat, row_start, n_rows):
    d = table.shape[1]
    n_workers = SC_CORES * SC_SUBCORES
    assert n_rows % (n_workers * SC_WINDOW) == 0
    per_worker = n_rows // n_workers
    mesh = plsc.VectorSubcoreMesh(core_axis_name="c", subcore_axis_name="s")

    @functools.partial(
        pl.kernel, mesh=mesh,
        out_type=jax.ShapeDtypeStruct((n_rows, d), table.dtype),
        scratch_types=[
            pltpu.VMEM((SC_WINDOW,), I32),
            pltpu.VMEM((SC_WINDOW, d), table.dtype),
            pltpu.SemaphoreType.DMA,
        ],
        name="sc_gather_rows",
    )
    def gather(table_hbm, idx_hbm, out_hbm, idx_v, rows_v, sem):
        base = (lax.axis_index("s") * SC_CORES + lax.axis_index("c")) * per_worker

        @pl.loop(0, per_worker // SC_WINDOW)
        def _(w):
            off = base + w * SC_WINDOW
            pltpu.sync_copy(idx_hbm.at[pl.ds(row_start + off, SC_WINDOW)], idx_v)
            pltpu.async_copy(table_hbm.at[idx_v], rows_v, sem).wait()
            pltpu.sync_copy(rows_v, out_hbm.at[pl.ds(off, SC_WINDOW)])

    return gather(table, idx_flat)


def _peer_staged_kernel(gate_ref, h_ref, x_ref, mod_ref, rows_ref, o_ref):
    part = pl.program_id(1)
    for j in range(PEER_STAGE_TOK):
        _reduce_token(part * PEER_STAGE_TOK + j, j, rows_ref, gate_ref, h_ref, x_ref, mod_ref, o_ref)


def _peer_experts_staged(rows, gate_t, h2, x2, gate, seq, tok_start, n_tok):
    d = x2.shape[1]
    nsel = PEER_HEADS * PEER_TOPK
    first = tok_start // PEER_TOK
    parts = PEER_TOK // PEER_STAGE_TOK
    return pl.pallas_call(
        _peer_staged_kernel,
        grid=(n_tok // PEER_TOK, parts),
        in_specs=[
            pl.BlockSpec((nsel, PEER_TOK), lambda i, p: (0, first + i)),
            pl.BlockSpec((PEER_TOK, d), lambda i, p: (first + i, 0)),
            pl.BlockSpec((PEER_TOK, d), lambda i, p: (first + i, 0)),
            pl.BlockSpec((1, 1, d), lambda i, p: (((first + i) * PEER_TOK) // seq, 0, 0)),
            pl.BlockSpec((PEER_STAGE_TOK * nsel, d), lambda i, p: (i * parts + p, 0)),
        ],
        out_specs=pl.BlockSpec((PEER_TOK, d), lambda i, p: (i, 0)),
        out_shape=jax.ShapeDtypeStruct((n_tok, d), F32),
        compiler_params=_params("arbitrary", "arbitrary"),
        name="peer_experts_staged",
    )(gate_t, h2, x2, gate, rows)


def _final_norm_kernel(x_ref, g_ref, o_ref):
    x = x_ref[...]
    ms = jnp.mean(x * x, axis=-1, keepdims=True)
    o_ref[...] = x * lax.rsqrt(ms + NORM_EPS) * g_ref[...]


def _final_norm(x2, g, tm=512):
    t, d = x2.shape
    return pl.pallas_call(
        _final_norm_kernel,
        grid=(t // tm,),
        in_specs=[pl.BlockSpec((tm, d), lambda i: (i, 0)), pl.BlockSpec((1, d), lambda i: (0, 0))],
        out_specs=pl.BlockSpec((tm, d), lambda i: (i, 0)),
        out_shape=jax.ShapeDtypeStruct((t, d), F32),
        compiler_params=_params("arbitrary"),
        name="final_norm",
    )(x2, g.reshape(1, d))


def _pack_expert_rows(u, v):
    hi = lax.bitcast_convert_type(u.astype(BF16), jnp.uint16).astype(jnp.uint32)
    lo = lax.bitcast_convert_type(v.astype(BF16), jnp.uint16).astype(jnp.uint32)
    return lax.bitcast_convert_type((hi << 16) | lo, I32)


def _reorder_in_cols(w):
    n_qkv = SWA_WIDTH + 2 * SWA_KV_WIDTH + 3 * SB_WIDTH
    return jnp.concatenate([w[:, n_qkv:], w[:, :n_qkv]], axis=1)


def kernel(x, c, positions, ada_w, ada_b, norm1_g, w_in, swa_sinks, w_branch_swa, w_branch_sb,
           w_out, norm2_g, peer_wq, peer_subkeys, peer_u, peer_v, final_g):
    b, s, d = x.shape
    depth = ada_w.shape[0]
    t = b * s
    mod = _ada_mod(c, ada_w, ada_b)
    pos3 = positions.reshape(b, s, 1)
    inv_freq = jnp.power(ROPE_THETA, -jnp.arange(HEAD_DIM // 2, dtype=F32) * (2.0 / HEAD_DIM))
    invf = jnp.tile(inv_freq, LANES // (HEAD_DIM // 2)).reshape(1, LANES)
    x2 = x.reshape(t, d)
    n_blocks = t // PEER_TOK
    tc_blocks = max(1, int(n_blocks * PEER_TC_SHARE))
    bounds = [tc_blocks + ((n_blocks - tc_blocks) * k) // PEER_SC_CHUNKS for k in range(PEER_SC_CHUNKS + 1)]
    for l in range(depth):
        m = mod[l].reshape(b, N_MOD, 1, d)
        shift1, scale1, gate1, shift2, scale2, gate2 = [m[:, i] for i in range(N_MOD)]
        w_in_l = _reorder_in_cols(w_in[l]).astype(BF16)
        proj = _norm_proj(x2, norm1_g[l], scale1, shift1, w_in_l, s)
        proj3 = proj.reshape(b, s, IN_COLS)
        y_a = _swa(proj3, pos3, invf, swa_sinks[l])
        y_b = _stick_breaking(proj3)
        x2 = _merge(y_a.reshape(t, SWA_WIDTH), y_b.reshape(t, SB_WIDTH), proj, x2, gate1,
                    w_branch_swa[l].astype(BF16), w_branch_sb[l].astype(BF16),
                    w_out[l].astype(BF16), s)
        sk = peer_subkeys[l].reshape(PEER_HEADS * 2, PEER_NKEYS, PEER_HALF).astype(BF16)
        h2, idx_t, gate_t = _peer_route(x2, norm2_g[l], scale2, shift2, peer_wq[l].astype(BF16), sk, s)
        table = _pack_expert_rows(peer_u[l], peer_v[l])
        idx_flat = idx_t.T.reshape(-1)
        nsel = PEER_HEADS * PEER_TOPK
        parts = [_peer_experts(idx_flat, gate_t, h2, x2, gate2, table[:, None, :], s,
                               bounds[0] * PEER_TOK)]
        for lo, hi in zip(bounds[:-1], bounds[1:]):
            start, n_tok = lo * PEER_TOK, (hi - lo) * PEER_TOK
            staged = _sc_gather_rows(table, idx_flat, start * nsel, n_tok * nsel)
            parts.append(_peer_experts_staged(staged, gate_t, h2, x2, gate2, s, start, n_tok))
        x2 = jnp.concatenate(parts, axis=0)
    return _final_norm(x2, final_g).reshape(b, s, d)
```

```python
import functools

import jax
import jax.numpy as jnp
import numpy as np
from jax import lax
from jax.experimental import pallas as pl
from jax.experimental.pallas import tpu as pltpu
from jax.experimental.pallas import tpu_sc as plsc

F32 = jnp.float32
BF16 = jnp.bfloat16
I32 = jnp.int32

HEAD_DIM = 64
BLOCK = 128
SWA_Q_HEADS = 8
SWA_KV_HEADS = 2
SB_HEADS = 8
ROPE_THETA = 10000.0
SWA_WIDTH = SWA_Q_HEADS * HEAD_DIM
SWA_KV_WIDTH = SWA_KV_HEADS * HEAD_DIM
SB_WIDTH = SB_HEADS * HEAD_DIM
PEER_HEADS = 8
PEER_NKEYS = 128
PEER_TOPK = 16
PEER_QDIM = 256
PEER_HALF = PEER_QDIM // 2
N_MOD = 6
NORM_EPS = 1e-6
NEG_INF = -1e30

LANES = 128
SUBLANES = 8
VMEM_LIMIT_BYTES = 56 * 1024 * 1024

COL_GATE_SWA = 0
COL_GATE_SB = 1024
COL_Q_SWA = 2048
COL_K_SWA = COL_Q_SWA + SWA_WIDTH
COL_V_SWA = COL_K_SWA + SWA_KV_WIDTH
COL_Q_SB = COL_V_SWA + SWA_KV_WIDTH
COL_K_SB = COL_Q_SB + SB_WIDTH
COL_V_SB = COL_K_SB + SB_WIDTH
IN_COLS = COL_V_SB + SB_WIDTH

SB_CHUNK = 2 * BLOCK
PEER_SUB = 4
PEER_TOK = 128
PEER_STAGE_TOK = 32
PEER_TC_SHARE = 0.34
PEER_SC_CHUNKS = 2

SC_CORES = 2
SC_SUBCORES = 16
SC_WINDOW = 64


def _dot(a, b):
    return jnp.dot(a, b, preferred_element_type=F32)


def _dot_nt(a, b):
    return lax.dot_general(a, b, (((1,), (1,)), ((), ())), preferred_element_type=F32)


def _split_bf16(a):
    hi = a.astype(BF16)
    lo = (a - hi.astype(F32)).astype(BF16)
    return hi, lo


def _params(*sem):
    return pltpu.CompilerParams(dimension_semantics=sem, vmem_limit_bytes=VMEM_LIMIT_BYTES)


def _resident(shape, index_map):
    return pl.BlockSpec(shape, index_map, pipeline_mode=pl.Buffered(1))


def _ada_kernel(c_ref, w_ref, b_ref, o_ref):
    c = c_ref[...]
    a = c * jax.nn.sigmoid(c)
    a_hi, a_lo = _split_bf16(a)
    w_hi, w_lo = _split_bf16(w_ref[0])
    acc = _dot(a_hi, w_hi) + _dot(a_hi, w_lo) + _dot(a_lo, w_hi)
    o_ref[0] = acc + b_ref[0]


def _ada_mod(c, ada_w, ada_b):
    depth, d, n = ada_w.shape
    b = c.shape[0]
    tn = n // 4
    return pl.pallas_call(
        _ada_kernel,
        grid=(depth, n // tn),
        in_specs=[
            pl.BlockSpec((b, d), lambda l, j: (0, 0)),
            pl.BlockSpec((1, d, tn), lambda l, j: (l, 0, j)),
            pl.BlockSpec((1, 1, tn), lambda l, j: (l, 0, j)),
        ],
        out_specs=pl.BlockSpec((1, b, tn), lambda l, j: (l, 0, j)),
        out_shape=jax.ShapeDtypeStruct((depth, b, n), F32),
        compiler_params=_params("arbitrary", "arbitrary"),
        name="ada_mod",
    )(c, ada_w, ada_b.reshape(depth, 1, n))


def _modulated_norm(x, g, scale, shift):
    ms = jnp.mean(x * x, axis=-1, keepdims=True)
    y = x * lax.rsqrt(ms + NORM_EPS) * g
    return y * (1.0 + scale) + shift


def _norm_proj_kernel(x_ref, g_ref, scale_ref, shift_ref, w_ref, o_ref, *, col_chunk):
    h = _modulated_norm(x_ref[...], g_ref[...], scale_ref[0], shift_ref[0]).astype(BF16)
    n = o_ref.shape[1]
    for c0 in range(0, n, col_chunk):
        c1 = min(c0 + col_chunk, n)
        o_ref[:, c0:c1] = _dot(h, w_ref[:, c0:c1]).astype(o_ref.dtype)


def _norm_proj(x2, g, scale, shift, w_bf16, seq, tm=512):
    t, d = x2.shape
    n = w_bf16.shape[1]
    per_batch = lambda i: ((i * tm) // seq, 0, 0)
    return pl.pallas_call(
        functools.partial(_norm_proj_kernel, col_chunk=1024),
        grid=(t // tm,),
        in_specs=[
            pl.BlockSpec((tm, d), lambda i: (i, 0)),
            pl.BlockSpec((1, d), lambda i: (0, 0)),
            pl.BlockSpec((1, 1, d), per_batch),
            pl.BlockSpec((1, 1, d), per_batch),
            _resident((d, n), lambda i: (0, 0)),
        ],
        out_specs=pl.BlockSpec((tm, n), lambda i: (i, 0)),
        out_shape=jax.ShapeDtypeStruct((t, n), BF16),
        compiler_params=_params("arbitrary"),
        name="norm_proj",
    )(x2, g.reshape(1, d), scale, shift, w_bf16)


def _swa_kernel(sink_ref, q_ref, kc_ref, kp_ref, vc_ref, vp_ref, pc_ref, pp_ref, invf_ref, o_ref):
    n = pl.program_id(1)
    lane = lax.broadcasted_iota(I32, (BLOCK, LANES), 1)
    first_half = (lane % HEAD_DIM) < (HEAD_DIM // 2)
    low_head = lane < HEAD_DIM
    invf = invf_ref[...]

    def rope_tables(pos_ref):
        ang = pos_ref[0].astype(F32) * invf
        s = jnp.sin(ang)
        return jnp.cos(ang), jnp.where(first_half, -s, s)

    def rope(x, tables):
        c, s_signed = tables
        swapped = jnp.where(first_half, pltpu.roll(x, LANES - HEAD_DIM // 2, 1),
                            pltpu.roll(x, HEAD_DIM // 2, 1))
        return x * c + swapped * s_signed

    tab_c = rope_tables(pc_ref)
    tab_p = rope_tables(pp_ref)
    k = jnp.concatenate([rope(kp_ref[0].astype(F32), tab_p), rope(kc_ref[0].astype(F32), tab_c)], axis=0)
    v = jnp.concatenate([vp_ref[0], vc_ref[0]], axis=0).astype(F32)
    k_var = (k.astype(BF16), pltpu.roll(k, HEAD_DIM, 1).astype(BF16))
    v_var = (v.astype(BF16), pltpu.roll(v, HEAD_DIM, 1).astype(BF16))

    row = lax.broadcasted_iota(I32, (BLOCK, 2 * BLOCK), 0)
    col = lax.broadcasted_iota(I32, (BLOCK, 2 * BLOCK), 1)
    diff = row + BLOCK - col
    valid = (diff >= 0) & (diff < BLOCK) & ((col >= BLOCK) | (n > 0))

    group = SWA_Q_HEADS // SWA_KV_HEADS
    for jt in range(SWA_Q_HEADS // 2):
        q_t = rope(q_ref[0, :, jt * LANES:(jt + 1) * LANES].astype(F32), tab_c) * (HEAD_DIM ** -0.5)
        outs = []
        for hh in range(2):
            j = 2 * jt + hh
            g = j // group
            qm = jnp.where(low_head == (hh == 0), q_t, 0.0).astype(BF16)
            s = _dot_nt(qm, k_var[0 if g == hh else 1])
            s = jnp.where(valid, s, NEG_INF)
            sink = sink_ref[j]
            m = jnp.maximum(jnp.max(s, axis=-1, keepdims=True), sink)
            p = jnp.exp(s - m)
            den = jnp.sum(p, axis=-1, keepdims=True) + jnp.exp(sink - m)
            outs.append(_dot(p.astype(BF16), v_var[0 if g == hh else 1]) / den)
        o_ref[0, :, jt * LANES:(jt + 1) * LANES] = jnp.where(low_head, outs[0], outs[1]).astype(o_ref.dtype)


def _swa(proj3, pos3, invf, sinks):
    b, s, _ = proj3.shape
    nb = s // BLOCK
    cur = lambda c: (lambda i, n: (i, n, c))
    prev = lambda c: (lambda i, n: (i, jnp.maximum(n - 1, 0), c))
    return pl.pallas_call(
        _swa_kernel,
        grid=(b, nb),
        in_specs=[
            pl.BlockSpec(memory_space=pltpu.SMEM),
            pl.BlockSpec((1, BLOCK, SWA_WIDTH), cur(COL_Q_SWA // SWA_WIDTH)),
            pl.BlockSpec((1, BLOCK, LANES), cur(COL_K_SWA // LANES)),
            pl.BlockSpec((1, BLOCK, LANES), prev(COL_K_SWA // LANES)),
            pl.BlockSpec((1, BLOCK, LANES), cur(COL_V_SWA // LANES)),
            pl.BlockSpec((1, BLOCK, LANES), prev(COL_V_SWA // LANES)),
            pl.BlockSpec((1, BLOCK, 1), cur(0)),
            pl.BlockSpec((1, BLOCK, 1), prev(0)),
            pl.BlockSpec((1, LANES), lambda i, n: (0, 0)),
        ],
        out_specs=pl.BlockSpec((1, BLOCK, SWA_WIDTH), lambda i, n: (i, n, 0)),
        out_shape=jax.ShapeDtypeStruct((b, s, SWA_WIDTH), BF16),
        compiler_params=_params("arbitrary", "arbitrary"),
        name="swa",
    )(sinks, proj3, proj3, proj3, proj3, proj3, pos3, pos3, invf)


def _sb_kernel(q_ref, k_ref, v_ref, tri_ref, o_ref):
    qb = pl.program_id(2)
    lane = lax.broadcasted_iota(I32, (SB_CHUNK, LANES), 1)
    low_head = lane < HEAD_DIM
    q = q_ref[0].astype(F32) * (HEAD_DIM ** -0.5)
    qms = [jnp.where(low_head == (h == 0), q, 0.0).astype(BF16) for h in range(2)]
    tri = tri_ref[...]
    row = lax.broadcasted_iota(I32, (SB_CHUNK, SB_CHUNK), 0)
    col = lax.broadcasted_iota(I32, (SB_CHUNK, SB_CHUNK), 1)
    strict = col < row

    def sweep_pair(c0, carry, diagonal):
        c1 = c0 - 1
        exists = c1 >= 0
        off0 = pl.multiple_of(c0 * SB_CHUNK, SB_CHUNK)
        off1 = pl.multiple_of(jnp.maximum(c1, 0) * SB_CHUNK, SB_CHUNK)
        k0 = k_ref[0, pl.ds(off0, SB_CHUNK), :]
        v0 = v_ref[0, pl.ds(off0, SB_CHUNK), :]
        k1 = k_ref[0, pl.ds(off1, SB_CHUNK), :]
        v1 = v_ref[0, pl.ds(off1, SB_CHUNK), :]
        v1 = jnp.where(exists, v1, jnp.zeros_like(v1))
        chains = [(h, kc, vc, diagonal and first) for h in range(2)
                  for kc, vc, first in ((k0, v0, True), (k1, v1, False))]
        zs = [_dot_nt(qms[h], kc) for h, kc, _, _ in chains]
        log_1m = []
        for z, (_, _, _, masked) in zip(zs, chains):
            neg = -z
            t = jnp.minimum(neg, 0.0) - jnp.log(1.0 + jnp.exp(jnp.minimum(z, neg)))
            log_1m.append(jnp.where(strict, t, 0.0) if masked else t)
        sums = [jnp.sum(t, axis=-1, keepdims=True) for t in log_1m]
        after = [_dot(t.astype(BF16), tri) for t in log_1m]
        new = []
        for h in range(2):
            acc, later = carry[2 * h], carry[2 * h + 1]
            for n, later_n in ((2 * h, later), (2 * h + 1, later + sums[2 * h])):
                attn = jnp.exp(zs[n] + log_1m[n] + after[n] + later_n)
                if chains[n][3]:
                    attn = jnp.where(strict, attn, 0.0)
                acc = acc + _dot(attn.astype(BF16), chains[n][2])
            new.append(acc)
            new.append(later + sums[2 * h] + sums[2 * h + 1])
        return tuple(new)

    zeros = (jnp.zeros((SB_CHUNK, LANES), F32), jnp.zeros((SB_CHUNK, 1), F32))
    carry = sweep_pair(qb, zeros + zeros, True)
    n_pairs = lax.shift_right_logical(qb + 2, 1)
    carry = lax.fori_loop(1, n_pairs, lambda i, c: sweep_pair(qb - 2 * i, c, False), carry)
    o_ref[0] = jnp.where(low_head, carry[0], carry[2]).astype(o_ref.dtype)


def _stick_breaking(proj3):
    b, s, _ = proj3.shape
    assert s % SB_CHUNK == 0
    nb = s // SB_CHUNK
    pairs = SB_HEADS // 2
    tri = (np.arange(SB_CHUNK)[:, None] > np.arange(SB_CHUNK)[None, :])
    tri = jnp.asarray(tri, dtype=BF16)
    return pl.pallas_call(
        _sb_kernel,
        grid=(b, pairs, nb),
        in_specs=[
            pl.BlockSpec((1, SB_CHUNK, LANES), lambda i, p, n: (i, n, COL_Q_SB // LANES + p)),
            pl.BlockSpec((1, s, LANES), lambda i, p, n: (i, 0, COL_K_SB // LANES + p)),
            pl.BlockSpec((1, s, LANES), lambda i, p, n: (i, 0, COL_V_SB // LANES + p)),
            pl.BlockSpec((SB_CHUNK, SB_CHUNK), lambda i, p, n: (0, 0)),
        ],
        out_specs=pl.BlockSpec((1, SB_CHUNK, LANES), lambda i, p, n: (i, n, p)),
        out_shape=jax.ShapeDtypeStruct((b, s, SB_WIDTH), BF16),
        compiler_params=_params("arbitrary", "arbitrary", "arbitrary"),
        name="stick_breaking",
    )(proj3, proj3, proj3, tri)


def _merge_kernel(ya_ref, yb_ref, ga_ref, gb_ref, x_ref, gate_ref, wa_ref, wb_ref, wo_ref, o_ref):
    ma = _dot(ya_ref[...], wa_ref[...])
    mb = _dot(yb_ref[...], wb_ref[...])
    merged = (jax.nn.sigmoid(ga_ref[...].astype(F32)) * ma
              + jax.nn.sigmoid(gb_ref[...].astype(F32)) * mb)
    o_ref[...] = x_ref[...] + gate_ref[0] * _dot(merged.astype(BF16), wo_ref[...])


def _merge(ya, yb, proj, x2, gate, wa, wb, wo, seq, tm=512):
    t, d = x2.shape
    per_batch = lambda i: ((i * tm) // seq, 0, 0)
    return pl.pallas_call(
        _merge_kernel,
        grid=(t // tm,),
        in_specs=[
            pl.BlockSpec((tm, SWA_WIDTH), lambda i: (i, 0)),
            pl.BlockSpec((tm, SB_WIDTH), lambda i: (i, 0)),
            pl.BlockSpec((tm, d), lambda i: (i, COL_GATE_SWA // d)),
            pl.BlockSpec((tm, d), lambda i: (i, COL_GATE_SB // d)),
            pl.BlockSpec((tm, d), lambda i: (i, 0)),
            pl.BlockSpec((1, 1, d), per_batch),
            _resident((SWA_WIDTH, d), lambda i: (0, 0)),
            _resident((SB_WIDTH, d), lambda i: (0, 0)),
            _resident((d, d), lambda i: (0, 0)),
        ],
        out_specs=pl.BlockSpec((tm, d), lambda i: (i, 0)),
        out_shape=jax.ShapeDtypeStruct((t, d), F32),
        compiler_params=_params("arbitrary"),
        name="merge_out",
    )(ya, yb, proj, proj, x2, gate, wa, wb, wo)


def _extract_top(vals, payload, k):
    r = vals.shape[0]
    idx = lax.broadcasted_iota(I32, vals.shape, 0)
    top_v, top_p = [], []
    for _ in range(k):
        m = jnp.max(vals, axis=0, keepdims=True)
        first = jnp.min(jnp.where(vals == m, idx, r), axis=0, keepdims=True)
        hit = idx == first
        top_v.append(m)
        if payload is None:
            top_p.append(first)
        else:
            top_p.append(jnp.sum(jnp.where(hit, payload, 0), axis=0, keepdims=True))
        vals = jnp.where(hit, -jnp.inf, vals)
    return top_v, top_p


def _peer_route_kernel(x_ref, g_ref, scale_ref, shift_ref, wq_ref, sk_ref, h_ref, idx_ref, gate_ref):
    h = _modulated_norm(x_ref[...], g_ref[...], scale_ref[0], shift_ref[0])
    h_ref[...] = h
    q = _dot(h.astype(BF16), wq_ref[...]).astype(BF16)
    idx_rows, gate_rows = [], []
    for head in range(PEER_HEADS):
        halves = []
        for p in range(2):
            c0 = (head * 2 + p) * PEER_HALF
            scores = _dot_nt(sk_ref[head * 2 + p], q[:, c0:c0 + PEER_HALF])
            halves.append(_extract_top(scores, None, PEER_TOPK))
        (s0, i0), (s1, i1) = halves
        s1_all = jnp.concatenate(s1, axis=0)
        i1_all = jnp.concatenate(i1, axis=0)
        cand_s, cand_i = [], []
        for i in range(PEER_TOPK):
            n_j = PEER_TOPK // (i + 1)
            rows = PEER_TOPK if n_j > SUBLANES else SUBLANES
            s_piece = s0[i] + s1_all[0:rows]
            i_piece = i0[i] * PEER_NKEYS + i1_all[0:rows]
            if n_j < rows:
                keep = lax.broadcasted_iota(I32, s_piece.shape, 0) < n_j
                s_piece = jnp.where(keep, s_piece, -jnp.inf)
            cand_s.append(s_piece)
            cand_i.append(i_piece)
        best_s, best_i = _extract_top(jnp.concatenate(cand_s, axis=0),
                                      jnp.concatenate(cand_i, axis=0), PEER_TOPK)
        e = [jnp.exp(s - best_s[0]) for s in best_s]
        den = e[0]
        for t in e[1:]:
            den = den + t
        idx_rows += best_i
        gate_rows += [t / den for t in e]
    idx_ref[...] = jnp.concatenate(idx_rows, axis=0)
    gate_ref[...] = jnp.concatenate(gate_rows, axis=0)


def _peer_route(x2, g, scale, shift, wq, sk, seq, tm=256):
    t, d = x2.shape
    nq = wq.shape[1]
    nsel = PEER_HEADS * PEER_TOPK
    per_batch = lambda i: ((i * tm) // seq, 0, 0)
    return pl.pallas_call(
        _peer_route_kernel,
        grid=(t // tm,),
        in_specs=[
            pl.BlockSpec((tm, d), lambda i: (i, 0)),
            pl.BlockSpec((1, d), lambda i: (0, 0)),
            pl.BlockSpec((1, 1, d), per_batch),
            pl.BlockSpec((1, 1, d), per_batch),
            _resident((d, nq), lambda i: (0, 0)),
            _resident(sk.shape, lambda i: (0, 0, 0)),
        ],
        out_specs=[
            pl.BlockSpec((tm, d), lambda i: (i, 0)),
            pl.BlockSpec((nsel, tm), lambda i: (0, i)),
            pl.BlockSpec((nsel, tm), lambda i: (0, i)),
        ],
        out_shape=[
            jax.ShapeDtypeStruct((t, d), F32),
            jax.ShapeDtypeStruct((nsel, t), I32),
            jax.ShapeDtypeStruct((nsel, t), F32),
        ],
        compiler_params=_params("arbitrary"),
        name="peer_route",
    )(x2, g.reshape(1, d), scale, shift, wq, sk)


def _reduce_token(tok, j, rows, gate_ref, h_ref, x_ref, mod_ref, o_ref):
    nsel = PEER_HEADS * PEER_TOPK
    h_row = h_ref[pl.ds(tok, 1), :]
    words = rows[j * nsel:(j + 1) * nsel, :]
    u_rows = lax.bitcast_convert_type(words & jnp.int32(-65536), F32)
    act = jnp.sum(u_rows * h_row, axis=-1, keepdims=True)
    gate_col = pltpu.roll(gate_ref[...], (PEER_TOK - tok) & (PEER_TOK - 1), 1)[:, 0:1]
    w = gate_col * (0.5 * act * (1.0 + lax.erf(act * np.float32(np.sqrt(0.5)))))
    v_rows = lax.bitcast_convert_type(words << 16, F32)
    out_row = jnp.sum(v_rows * w, axis=0, keepdims=True)
    o_ref[pl.ds(tok, 1), :] = x_ref[pl.ds(tok, 1), :] + mod_ref[0] * out_row


def _peer_expert_kernel(idx_ref, idx_next_ref, gate_ref, h_ref, x_ref, mod_ref, tab_ref, o_ref,
                        rows_a, rows_b, sem):
    i = pl.program_id(0)
    nsel = PEER_HEADS * PEER_TOPK
    n_sub = PEER_TOK // PEER_SUB
    bufs = ((rows_a, sem.at[0]), (rows_b, sem.at[1]))

    def row_copy(src_idx_ref, flat, dst, dst_sem, r):
        e = src_idx_ref[flat]
        return pltpu.make_async_copy(tab_ref.at[e], dst.at[pl.ds(r, 1)], dst_sem)

    def start_token(src_idx_ref, base, j, dst, dst_sem):
        for kq in range(nsel):
            row_copy(src_idx_ref, base + j * nsel + kq, dst, dst_sem, j * nsel + kq).start(priority=kq % 2)

    def wait_batch(dst, dst_sem):
        pltpu.make_async_copy(tab_ref.at[pl.ds(0, PEER_SUB * nsel), 0], dst, dst_sem).wait()

    def batch(s, parity, src_idx_ref, next_base):
        src, src_sem = bufs[parity]
        dst, dst_sem = bufs[1 - parity]
        wait_batch(src, src_sem)
        for j in range(PEER_SUB):
            start_token(src_idx_ref, next_base, j, dst, dst_sem)
            _reduce_token(s * PEER_SUB + j, j, src, gate_ref, h_ref, x_ref, mod_ref, o_ref)

    @pl.when(i == 0)
    def _():
        def first(r, _):
            row_copy(idx_ref, r, rows_a, sem.at[0], r).start()
            return 0
        lax.fori_loop(0, PEER_SUB * nsel, first, 0)

    def pair(p, _):
        s = 2 * p
        batch(s, 0, idx_ref, (s + 1) * PEER_SUB * nsel)
        batch(s + 1, 1, idx_ref, (s + 2) * PEER_SUB * nsel)
        return 0

    lax.fori_loop(0, n_sub // 2 - 1, pair, 0)
    batch(n_sub - 2, 0, idx_ref, (n_sub - 1) * PEER_SUB * nsel)
    batch(n_sub - 1, 1, idx_next_ref, 0)

    @pl.when(i == pl.num_programs(0) - 1)
    def _():
        wait_batch(rows_a, sem.at[0])


def _peer_experts(idx_flat, gate_t, h2, x2, gate, table3, seq, n_tok):
    d = x2.shape[1]
    nsel = PEER_HEADS * PEER_TOPK
    n_steps = n_tok // PEER_TOK
    per_batch = lambda i: ((i * PEER_TOK) // seq, 0, 0)
    idx_block = PEER_TOK * nsel
    return pl.pallas_call(
        _peer_expert_kernel,
        grid=(n_steps,),
        in_specs=[
            pl.BlockSpec((idx_block,), lambda i: (i,), memory_space=pltpu.SMEM),
            pl.BlockSpec((idx_block,), lambda i: (jnp.minimum(i + 1, n_steps - 1),),
                         memory_space=pltpu.SMEM),
            pl.BlockSpec((nsel, PEER_TOK), lambda i: (0, i)),
            pl.BlockSpec((PEER_TOK, d), lambda i: (i, 0)),
            pl.BlockSpec((PEER_TOK, d), lambda i: (i, 0)),
            pl.BlockSpec((1, 1, d), per_batch),
            pl.BlockSpec(memory_space=pl.ANY),
        ],
        out_specs=pl.BlockSpec((PEER_TOK, d), lambda i: (i, 0)),
        out_shape=jax.ShapeDtypeStruct((n_tok, d), F32),
        scratch_shapes=[
            pltpu.VMEM((PEER_SUB * nsel, d), I32),
            pltpu.VMEM((PEER_SUB * nsel, d), I32),
            pltpu.SemaphoreType.DMA((2,)),
        ],
        compiler_params=_params("arbitrary"),
        name="peer_experts",
    )(idx_flat, idx_flat, gate_t, h2, x2, gate, table3)


def _sc_gather_rows(table, idx_flat, row_start, n_rows):
    d = table.shape[1]
    n_workers = SC_CORES * SC_SUBCORES
    assert n_rows % (n_workers * SC_WINDOW) == 0
    per_worker = n_rows // n_workers
    mesh = plsc.VectorSubcoreMesh(core_axis_name="c", subcore_axis_name="s")

    @functools.partial(
        pl.kernel, mesh=mesh,
        out_type=jax.ShapeDtypeStruct((n_rows, d), table.dtype),
        scratch_types=[
            pltpu.VMEM((SC_WINDOW,), I32),
            pltpu.VMEM((SC_WINDOW, d), table.dtype),
            pltpu.SemaphoreType.DMA,
        ],
        name="sc_gather_rows",
    )
    def gather(table_hbm, idx_hbm, out_hbm, idx_v, rows_v, sem):
        base = (lax.axis_index("s") * SC_CORES + lax.axis_index("c")) * per_worker

        @pl.loop(0, per_worker // SC_WINDOW)
        def _(w):
            off = base + w * SC_WINDOW
            pltpu.sync_copy(idx_hbm.at[pl.ds(row_start + off, SC_WINDOW)], idx_v)
            pltpu.async_copy(table_hbm.at[idx_v], rows_v, sem).wait()
            pltpu.sync_copy(rows_v, out_hbm.at[pl.ds(off, SC_WINDOW)])

    return gather(table, idx_flat)


def _peer_staged_kernel(gate_ref, h_ref, x_ref, mod_ref, rows_ref, o_ref):
    part = pl.program_id(1)
    for j in range(PEER_STAGE_TOK):
        _reduce_token(part * PEER_STAGE_TOK + j, j, rows_ref, gate_ref, h_ref, x_ref, mod_ref, o_ref)


def _peer_experts_staged(rows, gate_t, h2, x2, gate, seq, tok_start, n_tok):
    d = x2.shape[1]
    nsel = PEER_HEADS * PEER_TOPK
    first = tok_start // PEER_TOK
    parts = PEER_TOK // PEER_STAGE_TOK
    return pl.pallas_call(
        _peer_staged_kernel,
        grid=(n_tok // PEER_TOK, parts),
        in_specs=[
            pl.BlockSpec((nsel, PEER_TOK), lambda i, p: (0, first + i)),
            pl.BlockSpec((PEER_TOK, d), lambda i, p: (first + i, 0)),
            pl.BlockSpec((PEER_TOK, d), lambda i, p: (first + i, 0)),
            pl.BlockSpec((1, 1, d), lambda i, p: (((first + i) * PEER_TOK) // seq, 0, 0)),
            pl.BlockSpec((PEER_STAGE_TOK * nsel, d), lambda i, p: (i * parts + p, 0)),
        ],
        out_specs=pl.BlockSpec((PEER_TOK, d), lambda i, p: (i, 0)),
        out_shape=jax.ShapeDtypeStruct((n_tok, d), F32),
        compiler_params=_params("arbitrary", "arbitrary"),
        name="peer_experts_staged",
    )(gate_t, h2, x2, gate, rows)


def _final_norm_kernel(x_ref, g_ref, o_ref):
    x = x_ref[...]
    ms = jnp.mean(x * x, axis=-1, keepdims=True)
    o_ref[...] = x * lax.rsqrt(ms + NORM_EPS) * g_ref[...]


def _final_norm(x2, g, tm=512):
    t, d = x2.shape
    return pl.pallas_call(
        _final_norm_kernel,
        grid=(t // tm,),
        in_specs=[pl.BlockSpec((tm, d), lambda i: (i, 0)), pl.BlockSpec((1, d), lambda i: (0, 0))],
        out_specs=pl.BlockSpec((tm, d), lambda i: (i, 0)),
        out_shape=jax.ShapeDtypeStruct((t, d), F32),
        compiler_params=_params("arbitrary"),
        name="final_norm",
    )(x2, g.reshape(1, d))


def _pack_expert_rows(u, v):
    hi = lax.bitcast_convert_type(u.astype(BF16), jnp.uint16).astype(jnp.uint32)
    lo = lax.bitcast_convert_type(v.astype(BF16), jnp.uint16).astype(jnp.uint32)
    return lax.bitcast_convert_type((hi << 16) | lo, I32)


def _reorder_in_cols(w):
    n_qkv = SWA_WIDTH + 2 * SWA_KV_WIDTH + 3 * SB_WIDTH
    return jnp.concatenate([w[:, n_qkv:], w[:, :n_qkv]], axis=1)


def kernel(x, c, positions, ada_w, ada_b, norm1_g, w_in, swa_sinks, w_branch_swa, w_branch_sb,
           w_out, norm2_g, peer_wq, peer_subkeys, peer_u, peer_v, final_g):
    b, s, d = x.shape
    depth = ada_w.shape[0]
    t = b * s
    mod = _ada_mod(c, ada_w, ada_b)
    pos3 = positions.reshape(b, s, 1)
    inv_freq = jnp.power(ROPE_THETA, -jnp.arange(HEAD_DIM // 2, dtype=F32) * (2.0 / HEAD_DIM))
    invf = jnp.tile(inv_freq, LANES // (HEAD_DIM // 2)).reshape(1, LANES)
    x2 = x.reshape(t, d)
    n_blocks = t // PEER_TOK
    tc_blocks = max(1, int(n_blocks * PEER_TC_SHARE))
    bounds = [tc_blocks + ((n_blocks - tc_blocks) * k) // PEER_SC_CHUNKS for k in range(PEER_SC_CHUNKS + 1)]
    for l in range(depth):
        m = mod[l].reshape(b, N_MOD, 1, d)
        shift1, scale1, gate1, shift2, scale2, gate2 = [m[:, i] for i in range(N_MOD)]
        w_in_l = _reorder_in_cols(w_in[l]).astype(BF16)
        proj = _norm_proj(x2, norm1_g[l], scale1, shift1, w_in_l, s)
        proj3 = proj.reshape(b, s, IN_COLS)
        y_a = _swa(proj3, pos3, invf, swa_sinks[l])
        y_b = _stick_breaking(proj3)
        x2 = _merge(y_a.reshape(t, SWA_WIDTH), y_b.reshape(t, SB_WIDTH), proj, x2, gate1,
                    w_branch_swa[l].astype(BF16), w_branch_sb[l].astype(BF16),
                    w_out[l].astype(BF16), s)
        sk = peer_subkeys[l].reshape(PEER_HEADS * 2, PEER_NKEYS, PEER_HALF).astype(BF16)
        h2, idx_t, gate_t = _peer_route(x2, norm2_g[l], scale2, shift2, peer_wq[l].astype(BF16), sk, s)
        table = _pack_expert_rows(peer_u[l], peer_v[l])
        idx_flat = idx_t.T.reshape(-1)
        nsel = PEER_HEADS * PEER_TOPK
        parts = [_peer_experts(idx_flat, gate_t, h2, x2, gate2, table[:, None, :], s,
                               bounds[0] * PEER_TOK)]
        for lo, hi in zip(bounds[:-1], bounds[1:]):
            start, n_tok = lo * PEER_TOK, (hi - lo) * PEER_TOK
            staged = _sc_gather_rows(table, idx_flat, start * nsel, n_tok * nsel)
            parts.append(_peer_experts_staged(staged, gate_t, h2, x2, gate2, s, start, n_tok))
        x2 = jnp.concatenate(parts, axis=0)
    return _final_norm(x2, final_g).reshape(b, s, d)
```

```python
import functools

import jax
import jax.numpy as jnp
import numpy as np
from jax import lax
from jax.experimental import pallas as pl
from jax.experimental.pallas import tpu as pltpu
from jax.experimental.pallas import tpu_sc as plsc

F32 = jnp.float32
BF16 = jnp.bfloat16
I32 = jnp.int32

HEAD_DIM = 64
BLOCK = 128
SWA_Q_HEADS = 8
SWA_KV_HEADS = 2
SB_HEADS = 8
ROPE_THETA = 10000.0
SWA_WIDTH = SWA_Q_HEADS * HEAD_DIM
SWA_KV_WIDTH = SWA_KV_HEADS * HEAD_DIM
SB_WIDTH = SB_HEADS * HEAD_DIM
PEER_HEADS = 8
PEER_NKEYS = 128
PEER_TOPK = 16
PEER_QDIM = 256
PEER_HALF = PEER_QDIM // 2
N_MOD = 6
NORM_EPS = 1e-6
NEG_INF = -1e30

LANES = 128
SUBLANES = 8
VMEM_LIMIT_BYTES = 56 * 1024 * 1024

COL_GATE_SWA = 0
COL_GATE_SB = 1024
COL_Q_SWA = 2048
COL_K_SWA = COL_Q_SWA + SWA_WIDTH
COL_V_SWA = COL_K_SWA + SWA_KV_WIDTH
COL_Q_SB = COL_V_SWA + SWA_KV_WIDTH
COL_K_SB = COL_Q_SB + SB_WIDTH
COL_V_SB = COL_K_SB + SB_WIDTH
IN_COLS = COL_V_SB + SB_WIDTH

SB_CHUNK = 2 * BLOCK
PEER_SUB = 4
PEER_TOK = 128
PEER_STAGE_TOK = 32
PEER_TC_SHARE = 0.3
PEER_SC_CHUNKS = 2
PEER_ROUTE_TOK = 256

SC_CORES = 2
SC_SUBCORES = 16
SC_WINDOW = 64


def _dot(a, b):
    return jnp.dot(a, b, preferred_element_type=F32)


def _dot_nt(a, b):
    return lax.dot_general(a, b, (((1,), (1,)), ((), ())), preferred_element_type=F32)


def _split_bf16(a):
    hi = a.astype(BF16)
    lo = (a - hi.astype(F32)).astype(BF16)
    return hi, lo


def _params(*sem):
    return pltpu.CompilerParams(dimension_semantics=sem, vmem_limit_bytes=VMEM_LIMIT_BYTES)


def _resident(shape, index_map):
    return pl.BlockSpec(shape, index_map, pipeline_mode=pl.Buffered(1))


def _ada_kernel(c_ref, w_ref, b_ref, o_ref):
    c = c_ref[...]
    a = c * jax.nn.sigmoid(c)
    a_hi, a_lo = _split_bf16(a)
    w_hi, w_lo = _split_bf16(w_ref[0])
    acc = _dot(a_hi, w_hi) + _dot(a_hi, w_lo) + _dot(a_lo, w_hi)
    o_ref[0] = acc + b_ref[0]


def _ada_mod(c, ada_w, ada_b):
    depth, d, n = ada_w.shape
    b = c.shape[0]
    tn = n // 4
    return pl.pallas_call(
        _ada_kernel,
        grid=(depth, n // tn),
        in_specs=[
            pl.BlockSpec((b, d), lambda l, j: (0, 0)),
            pl.BlockSpec((1, d, tn), lambda l, j: (l, 0, j)),
            pl.BlockSpec((1, 1, tn), lambda l, j: (l, 0, j)),
        ],
        out_specs=pl.BlockSpec((1, b, tn), lambda l, j: (l, 0, j)),
        out_shape=jax.ShapeDtypeStruct((depth, b, n), F32),
        compiler_params=_params("arbitrary", "arbitrary"),
        name="ada_mod",
    )(c, ada_w, ada_b.reshape(depth, 1, n))


def _modulated_norm(x, g, scale, shift):
    ms = jnp.mean(x * x, axis=-1, keepdims=True)
    y = x * lax.rsqrt(ms + NORM_EPS) * g
    return y * (1.0 + scale) + shift


def _norm_proj_kernel(x_ref, g_ref, scale_ref, shift_ref, w_ref, o_ref, *, col_chunk):
    h = _modulated_norm(x_ref[...], g_ref[...], scale_ref[0], shift_ref[0]).astype(BF16)
    n = o_ref.shape[1]
    for c0 in range(0, n, col_chunk):
        c1 = min(c0 + col_chunk, n)
        o_ref[:, c0:c1] = _dot(h, w_ref[:, c0:c1]).astype(o_ref.dtype)


def _norm_proj(x2, g, scale, shift, w_bf16, seq, tm=512):
    t, d = x2.shape
    n = w_bf16.shape[1]
    per_batch = lambda i: ((i * tm) // seq, 0, 0)
    return pl.pallas_call(
        functools.partial(_norm_proj_kernel, col_chunk=1024),
        grid=(t // tm,),
        in_specs=[
            pl.BlockSpec((tm, d), lambda i: (i, 0)),
            pl.BlockSpec((1, d), lambda i: (0, 0)),
            pl.BlockSpec((1, 1, d), per_batch),
            pl.BlockSpec((1, 1, d), per_batch),
            _resident((d, n), lambda i: (0, 0)),
        ],
        out_specs=pl.BlockSpec((tm, n), lambda i: (i, 0)),
        out_shape=jax.ShapeDtypeStruct((t, n), BF16),
        compiler_params=_params("arbitrary"),
        name="norm_proj",
    )(x2, g.reshape(1, d), scale, shift, w_bf16)


def _swa_kernel(sink_ref, q_ref, kc_ref, kp_ref, vc_ref, vp_ref, pc_ref, pp_ref, invf_ref, o_ref):
    n = pl.program_id(1)
    lane = lax.broadcasted_iota(I32, (BLOCK, LANES), 1)
    first_half = (lane % HEAD_DIM) < (HEAD_DIM // 2)
    low_head = lane < HEAD_DIM
    invf = invf_ref[...]

    def rope_tables(pos_ref):
        ang = pos_ref[0].astype(F32) * invf
        s = jnp.sin(ang)
        return jnp.cos(ang), jnp.where(first_half, -s, s)

    def rope(x, tables):
        c, s_signed = tables
        swapped = jnp.where(first_half, pltpu.roll(x, LANES - HEAD_DIM // 2, 1),
                            pltpu.roll(x, HEAD_DIM // 2, 1))
        return x * c + swapped * s_signed

    tab_c = rope_tables(pc_ref)
    tab_p = rope_tables(pp_ref)
    k = jnp.concatenate([rope(kp_ref[0].astype(F32), tab_p), rope(kc_ref[0].astype(F32), tab_c)], axis=0)
    v = jnp.concatenate([vp_ref[0], vc_ref[0]], axis=0).astype(F32)
    k_var = (k.astype(BF16), pltpu.roll(k, HEAD_DIM, 1).astype(BF16))
    v_var = (v.astype(BF16), pltpu.roll(v, HEAD_DIM, 1).astype(BF16))

    row = lax.broadcasted_iota(I32, (BLOCK, 2 * BLOCK), 0)
    col = lax.broadcasted_iota(I32, (BLOCK, 2 * BLOCK), 1)
    diff = row + BLOCK - col
    valid = (diff >= 0) & (diff < BLOCK) & ((col >= BLOCK) | (n > 0))

    group = SWA_Q_HEADS // SWA_KV_HEADS
    for jt in range(SWA_Q_HEADS // 2):
        q_t = rope(q_ref[0, :, jt * LANES:(jt + 1) * LANES].astype(F32), tab_c) * (HEAD_DIM ** -0.5)
        outs = []
        for hh in range(2):
            j = 2 * jt + hh
            g = j // group
            qm = jnp.where(low_head == (hh == 0), q_t, 0.0).astype(BF16)
            s = _dot_nt(qm, k_var[0 if g == hh else 1])
            s = jnp.where(valid, s, NEG_INF)
            sink = sink_ref[j]
            m = jnp.maximum(jnp.max(s, axis=-1, keepdims=True), sink)
            p = jnp.exp(s - m)
            den = jnp.sum(p, axis=-1, keepdims=True) + jnp.exp(sink - m)
            outs.append(_dot(p.astype(BF16), v_var[0 if g == hh else 1]) / den)
        o_ref[0, :, jt * LANES:(jt + 1) * LANES] = jnp.where(low_head, outs[0], outs[1]).astype(o_ref.dtype)


def _swa(proj3, pos3, invf, sinks):
    b, s, _ = proj3.shape
    nb = s // BLOCK
    cur = lambda c: (lambda i, n: (i, n, c))
    prev = lambda c: (lambda i, n: (i, jnp.maximum(n - 1, 0), c))
    return pl.pallas_call(
        _swa_kernel,
        grid=(b, nb),
        in_specs=[
            pl.BlockSpec(memory_space=pltpu.SMEM),
            pl.BlockSpec((1, BLOCK, SWA_WIDTH), cur(COL_Q_SWA // SWA_WIDTH)),
            pl.BlockSpec((1, BLOCK, LANES), cur(COL_K_SWA // LANES)),
            pl.BlockSpec((1, BLOCK, LANES), prev(COL_K_SWA // LANES)),
            pl.BlockSpec((1, BLOCK, LANES), cur(COL_V_SWA // LANES)),
            pl.BlockSpec((1, BLOCK, LANES), prev(COL_V_SWA // LANES)),
            pl.BlockSpec((1, BLOCK, 1), cur(0)),
            pl.BlockSpec((1, BLOCK, 1), prev(0)),
            pl.BlockSpec((1, LANES), lambda i, n: (0, 0)),
        ],
        out_specs=pl.BlockSpec((1, BLOCK, SWA_WIDTH), lambda i, n: (i, n, 0)),
        out_shape=jax.ShapeDtypeStruct((b, s, SWA_WIDTH), BF16),
        compiler_params=_params("arbitrary", "arbitrary"),
        name="swa",
    )(sinks, proj3, proj3, proj3, proj3, proj3, pos3, pos3, invf)


def _sb_kernel(q_ref, k_ref, v_ref, tri_ref, o_ref):
    qb = pl.program_id(2)
    lane = lax.broadcasted_iota(I32, (SB_CHUNK, LANES), 1)
    low_head = lane < HEAD_DIM
    q = q_ref[0].astype(F32) * (HEAD_DIM ** -0.5)
    qms = [jnp.where(low_head == (h == 0), q, 0.0).astype(BF16) for h in range(2)]
    tri = tri_ref[...]
    row = lax.broadcasted_iota(I32, (SB_CHUNK, SB_CHUNK), 0)
    col = lax.broadcasted_iota(I32, (SB_CHUNK, SB_CHUNK), 1)
    strict = col < row

    def sweep_pair(c0, carry, diagonal):
        c1 = c0 - 1
        exists = c1 >= 0
        off0 = pl.multiple_of(c0 * SB_CHUNK, SB_CHUNK)
        off1 = pl.multiple_of(jnp.maximum(c1, 0) * SB_CHUNK, SB_CHUNK)
        k0 = k_ref[0, pl.ds(off0, SB_CHUNK), :]
        v0 = v_ref[0, pl.ds(off0, SB_CHUNK), :]
        k1 = k_ref[0, pl.ds(off1, SB_CHUNK), :]
        v1 = v_ref[0, pl.ds(off1, SB_CHUNK), :]
        v1 = jnp.where(exists, v1, jnp.zeros_like(v1))
        chains = [(h, kc, vc, diagonal and first) for h in range(2)
                  for kc, vc, first in ((k0, v0, True), (k1, v1, False))]
        zs = [_dot_nt(qms[h], kc) for h, kc, _, _ in chains]
        log_1m = []
        for z, (_, _, _, masked) in zip(zs, chains):
            neg = -z
            t = jnp.minimum(neg, 0.0) - jnp.log(1.0 + jnp.exp(jnp.minimum(z, neg)))
            log_1m.append(jnp.where(strict, t, 0.0) if masked else t)
        sums = [jnp.sum(t, axis=-1, keepdims=True) for t in log_1m]
        after = [_dot(t.astype(BF16), tri) for t in log_1m]
        new = []
        for h in range(2):
            acc, later = carry[2 * h], carry[2 * h + 1]
            for n, later_n in ((2 * h, later), (2 * h + 1, later + sums[2 * h])):
                attn = jnp.exp(zs[n] + log_1m[n] + after[n] + later_n)
                if chains[n][3]:
                    attn = jnp.where(strict, attn, 0.0)
                acc = acc + _dot(attn.astype(BF16), chains[n][2])
            new.append(acc)
            new.append(later + sums[2 * h] + sums[2 * h + 1])
        return tuple(new)

    zeros = (jnp.zeros((SB_CHUNK, LANES), F32), jnp.zeros((SB_CHUNK, 1), F32))
    carry = sweep_pair(qb, zeros + zeros, True)
    n_pairs = lax.shift_right_logical(qb + 2, 1)
    carry = lax.fori_loop(1, n_pairs, lambda i, c: sweep_pair(qb - 2 * i, c, False), carry)
    o_ref[0] = jnp.where(low_head, carry[0], carry[2]).astype(o_ref.dtype)


def _stick_breaking(proj3):
    b, s, _ = proj3.shape
    assert s % SB_CHUNK == 0
    nb = s // SB_CHUNK
    pairs = SB_HEADS // 2
    tri = (np.arange(SB_CHUNK)[:, None] > np.arange(SB_CHUNK)[None, :])
    tri = jnp.asarray(tri, dtype=BF16)
    return pl.pallas_call(
        _sb_kernel,
        grid=(b, pairs, nb),
        in_specs=[
            pl.BlockSpec((1, SB_CHUNK, LANES), lambda i, p, n: (i, n, COL_Q_SB // LANES + p)),
            pl.BlockSpec((1, s, LANES), lambda i, p, n: (i, 0, COL_K_SB // LANES + p)),
            pl.BlockSpec((1, s, LANES), lambda i, p, n: (i, 0, COL_V_SB // LANES + p)),
            pl.BlockSpec((SB_CHUNK, SB_CHUNK), lambda i, p, n: (0, 0)),
        ],
        out_specs=pl.BlockSpec((1, SB_CHUNK, LANES), lambda i, p, n: (i, n, p)),
        out_shape=jax.ShapeDtypeStruct((b, s, SB_WIDTH), BF16),
        compiler_params=_params("arbitrary", "arbitrary", "arbitrary"),
        name="stick_breaking",
    )(proj3, proj3, proj3, tri)


def _merge_kernel(ya_ref, yb_ref, ga_ref, gb_ref, x_ref, gate_ref, wa_ref, wb_ref, wo_ref, o_ref):
    ma = _dot(ya_ref[...], wa_ref[...])
    mb = _dot(yb_ref[...], wb_ref[...])
    merged = (jax.nn.sigmoid(ga_ref[...].astype(F32)) * ma
              + jax.nn.sigmoid(gb_ref[...].astype(F32)) * mb)
    o_ref[...] = x_ref[...] + gate_ref[0] * _dot(merged.astype(BF16), wo_ref[...])


def _merge(ya, yb, proj, x2, gate, wa, wb, wo, seq, tm=512):
    t, d = x2.shape
    per_batch = lambda i: ((i * tm) // seq, 0, 0)
    return pl.pallas_call(
        _merge_kernel,
        grid=(t // tm,),
        in_specs=[
            pl.BlockSpec((tm, SWA_WIDTH), lambda i: (i, 0)),
            pl.BlockSpec((tm, SB_WIDTH), lambda i: (i, 0)),
            pl.BlockSpec((tm, d), lambda i: (i, COL_GATE_SWA // d)),
            pl.BlockSpec((tm, d), lambda i: (i, COL_GATE_SB // d)),
            pl.BlockSpec((tm, d), lambda i: (i, 0)),
            pl.BlockSpec((1, 1, d), per_batch),
            _resident((SWA_WIDTH, d), lambda i: (0, 0)),
            _resident((SB_WIDTH, d), lambda i: (0, 0)),
            _resident((d, d), lambda i: (0, 0)),
        ],
        out_specs=pl.BlockSpec((tm, d), lambda i: (i, 0)),
        out_shape=jax.ShapeDtypeStruct((t, d), F32),
        compiler_params=_params("arbitrary"),
        name="merge_out",
    )(ya, yb, proj, proj, x2, gate, wa, wb, wo)


def _extract_top(vals, payload, k):
    r = vals.shape[0]
    idx = lax.broadcasted_iota(I32, vals.shape, 0)
    top_v, top_p = [], []
    for _ in range(k):
        m = jnp.max(vals, axis=0, keepdims=True)
        first = jnp.min(jnp.where(vals == m, idx, r), axis=0, keepdims=True)
        hit = idx == first
        top_v.append(m)
        if payload is None:
            top_p.append(first)
        else:
            top_p.append(jnp.sum(jnp.where(hit, payload, 0), axis=0, keepdims=True))
        vals = jnp.where(hit, -jnp.inf, vals)
    return top_v, top_p


def _peer_route_kernel(x_ref, g_ref, scale_ref, shift_ref, wq_ref, sk_ref, h_ref, idx_ref, gate_ref):
    h = _modulated_norm(x_ref[...], g_ref[...], scale_ref[0], shift_ref[0])
    h_ref[...] = h
    q = _dot(h.astype(BF16), wq_ref[...]).astype(BF16)
    idx_rows, gate_rows = [], []
    for head in range(PEER_HEADS):
        halves = []
        for p in range(2):
            c0 = (head * 2 + p) * PEER_HALF
            scores = _dot_nt(sk_ref[head * 2 + p], q[:, c0:c0 + PEER_HALF])
            halves.append(_extract_top(scores, None, PEER_TOPK))
        (s0, i0), (s1, i1) = halves
        s1_all = jnp.concatenate(s1, axis=0)
        i1_all = jnp.concatenate(i1, axis=0)
        cand_s, cand_i = [], []
        for i in range(PEER_TOPK):
            n_j = PEER_TOPK // (i + 1)
            rows = PEER_TOPK if n_j > SUBLANES else SUBLANES
            s_piece = s0[i] + s1_all[0:rows]
            i_piece = i0[i] * PEER_NKEYS + i1_all[0:rows]
            if n_j < rows:
                keep = lax.broadcasted_iota(I32, s_piece.shape, 0) < n_j
                s_piece = jnp.where(keep, s_piece, -jnp.inf)
            cand_s.append(s_piece)
            cand_i.append(i_piece)
        best_s, best_i = _extract_top(jnp.concatenate(cand_s, axis=0),
                                      jnp.concatenate(cand_i, axis=0), PEER_TOPK)
        e = [jnp.exp(s - best_s[0]) for s in best_s]
        den = e[0]
        for t in e[1:]:
            den = den + t
        idx_rows += best_i
        gate_rows += [t / den for t in e]
    idx_ref[...] = jnp.concatenate(idx_rows, axis=0)
    gate_ref[...] = jnp.concatenate(gate_rows, axis=0)


def _peer_route(x2, g, scale, shift, wq, sk, seq, tok_start, n_tok, tm=PEER_ROUTE_TOK):
    d = x2.shape[1]
    nq = wq.shape[1]
    nsel = PEER_HEADS * PEER_TOPK
    first = tok_start // tm
    per_batch = lambda i: (((first + i) * tm) // seq, 0, 0)
    return pl.pallas_call(
        _peer_route_kernel,
        grid=(n_tok // tm,),
        in_specs=[
            pl.BlockSpec((tm, d), lambda i: (first + i, 0)),
            pl.BlockSpec((1, d), lambda i: (0, 0)),
            pl.BlockSpec((1, 1, d), per_batch),
            pl.BlockSpec((1, 1, d), per_batch),
            _resident((d, nq), lambda i: (0, 0)),
            _resident(sk.shape, lambda i: (0, 0, 0)),
        ],
        out_specs=[
            pl.BlockSpec((tm, d), lambda i: (i, 0)),
            pl.BlockSpec((nsel, tm), lambda i: (0, i)),
            pl.BlockSpec((nsel, tm), lambda i: (0, i)),
        ],
        out_shape=[
            jax.ShapeDtypeStruct((n_tok, d), F32),
            jax.ShapeDtypeStruct((nsel, n_tok), I32),
            jax.ShapeDtypeStruct((nsel, n_tok), F32),
        ],
        compiler_params=_params("arbitrary"),
        name="peer_route",
    )(x2, g.reshape(1, d), scale, shift, wq, sk)


def _reduce_token(tok, j, rows, gate_ref, h_ref, x_ref, mod_ref, o_ref):
    nsel = PEER_HEADS * PEER_TOPK
    h_row = h_ref[pl.ds(tok, 1), :]
    words = rows[j * nsel:(j + 1) * nsel, :]
    u_rows = lax.bitcast_convert_type(words & jnp.int32(-65536), F32)
    act = jnp.sum(u_rows * h_row, axis=-1, keepdims=True)
    gate_col = pltpu.roll(gate_ref[...], (PEER_TOK - tok) & (PEER_TOK - 1), 1)[:, 0:1]
    w = gate_col * (0.5 * act * (1.0 + lax.erf(act * np.float32(np.sqrt(0.5)))))
    v_rows = lax.bitcast_convert_type(words << 16, F32)
    out_row = jnp.sum(v_rows * w, axis=0, keepdims=True)
    o_ref[pl.ds(tok, 1), :] = x_ref[pl.ds(tok, 1), :] + mod_ref[0] * out_row


def _peer_expert_kernel(idx_ref, idx_next_ref, gate_ref, h_ref, x_ref, mod_ref, tab_ref, o_ref,
                        rows_a, rows_b, sem):
    i = pl.program_id(0)
    nsel = PEER_HEADS * PEER_TOPK
    n_sub = PEER_TOK // PEER_SUB
    bufs = ((rows_a, sem.at[0]), (rows_b, sem.at[1]))

    def row_copy(src_idx_ref, flat, dst, dst_sem, r):
        e = src_idx_ref[flat]
        return pltpu.make_async_copy(tab_ref.at[e], dst.at[pl.ds(r, 1)], dst_sem)

    def start_token(src_idx_ref, base, j, dst, dst_sem):
        for kq in range(nsel):
            row_copy(src_idx_ref, base + j * nsel + kq, dst, dst_sem, j * nsel + kq).start(priority=kq % 2)

    def wait_batch(dst, dst_sem):
        pltpu.make_async_copy(tab_ref.at[pl.ds(0, PEER_SUB * nsel), 0], dst, dst_sem).wait()

    def batch(s, parity, src_idx_ref, next_base):
        src, src_sem = bufs[parity]
        dst, dst_sem = bufs[1 - parity]
        wait_batch(src, src_sem)
        for j in range(PEER_SUB):
            start_token(src_idx_ref, next_base, j, dst, dst_sem)
            _reduce_token(s * PEER_SUB + j, j, src, gate_ref, h_ref, x_ref, mod_ref, o_ref)

    @pl.when(i == 0)
    def _():
        def first(r, _):
            row_copy(idx_ref, r, rows_a, sem.at[0], r).start()
            return 0
        lax.fori_loop(0, PEER_SUB * nsel, first, 0)

    def pair(p, _):
        s = 2 * p
        batch(s, 0, idx_ref, (s + 1) * PEER_SUB * nsel)
        batch(s + 1, 1, idx_ref, (s + 2) * PEER_SUB * nsel)
        return 0

    lax.fori_loop(0, n_sub // 2 - 1, pair, 0)
    batch(n_sub - 2, 0, idx_ref, (n_sub - 1) * PEER_SUB * nsel)
    batch(n_sub - 1, 1, idx_next_ref, 0)

    @pl.when(i == pl.num_programs(0) - 1)
    def _():
        wait_batch(rows_a, sem.at[0])


def _peer_experts(idx_flat, gate_t, h2, x2, gate, table3, seq, n_tok):
    d = x2.shape[1]
    nsel = PEER_HEADS * PEER_TOPK
    n_steps = n_tok // PEER_TOK
    per_batch = lambda i: ((i * PEER_TOK) // seq, 0, 0)
    idx_block = PEER_TOK * nsel
    return pl.pallas_call(
        _peer_expert_kernel,
        grid=(n_steps,),
        in_specs=[
            pl.BlockSpec((idx_block,), lambda i: (i,), memory_space=pltpu.SMEM),
            pl.BlockSpec((idx_block,), lambda i: (jnp.minimum(i + 1, n_steps - 1),),
                         memory_space=pltpu.SMEM),
            pl.BlockSpec((nsel, PEER_TOK), lambda i: (0, i)),
            pl.BlockSpec((PEER_TOK, d), lambda i: (i, 0)),
            pl.BlockSpec((PEER_TOK, d), lambda i: (i, 0)),
            pl.BlockSpec((1, 1, d), per_batch),
            pl.BlockSpec(memory_space=pl.ANY),
        ],
        out_specs=pl.BlockSpec((PEER_TOK, d), lambda i: (i, 0)),
        out_shape=jax.ShapeDtypeStruct((n_tok, d), F32),
        scratch_shapes=[
            pltpu.VMEM((PEER_SUB * nsel, d), I32),
            pltpu.VMEM((PEER_SUB * nsel, d), I32),
            pltpu.SemaphoreType.DMA((2,)),
        ],
        compiler_params=_params("arbitrary"),
        name="peer_experts",
    )(idx_flat, idx_flat, gate_t, h2, x2, gate, table3)


def _sc_gather_rows(table, idx_flat, row_start, n_rows):
    d = table.shape[1]
    n_workers = SC_CORES * SC_SUBCORES
    assert n_rows % (n_workers * SC_WINDOW) == 0
    per_worker = n_rows // n_workers
    mesh = plsc.VectorSubcoreMesh(core_axis_name="c", subcore_axis_name="s")

    @functools.partial(
        pl.kernel, mesh=mesh,
        out_type=jax.ShapeDtypeStruct((n_rows, d), table.dtype),
        scratch_types=[
            pltpu.VMEM((SC_WINDOW,), I32),
            pltpu.VMEM((SC_WINDOW, d), table.dtype),
            pltpu.SemaphoreType.DMA,
        ],
        name="sc_gather_rows",
    )
    def gather(table_hbm, idx_hbm, out_hbm, idx_v, rows_v, sem):
        base = (lax.axis_index("s") * SC_CORES + lax.axis_index("c")) * per_worker

        @pl.loop(0, per_worker // SC_WINDOW)
        def _(w):
            off = base + w * SC_WINDOW
            pltpu.sync_copy(idx_hbm.at[pl.ds(row_start + off, SC_WINDOW)], idx_v)
            pltpu.async_copy(table_hbm.at[idx_v], rows_v, sem).wait()
            pltpu.sync_copy(rows_v, out_hbm.at[pl.ds(off, SC_WINDOW)])

    return gather(table, idx_flat)


def _peer_staged_kernel(gate_ref, h_ref, x_ref, mod_ref, rows_ref, o_ref):
    part = pl.program_id(1)
    for j in range(PEER_STAGE_TOK):
        _reduce_token(part * PEER_STAGE_TOK + j, j, rows_ref, gate_ref, h_ref, x_ref, mod_ref, o_ref)


def _peer_experts_staged(rows, gate_t, h2, x2, gate, seq, tok_start, n_tok):
    d = x2.shape[1]
    nsel = PEER_HEADS * PEER_TOPK
    first = tok_start // PEER_TOK
    parts = PEER_TOK // PEER_STAGE_TOK
    return pl.pallas_call(
        _peer_staged_kernel,
        grid=(n_tok // PEER_TOK, parts),
        in_specs=[
            pl.BlockSpec((nsel, PEER_TOK), lambda i, p: (0, i)),
            pl.BlockSpec((PEER_TOK, d), lambda i, p: (i, 0)),
            pl.BlockSpec((PEER_TOK, d), lambda i, p: (first + i, 0)),
            pl.BlockSpec((1, 1, d), lambda i, p: (((first + i) * PEER_TOK) // seq, 0, 0)),
            pl.BlockSpec((PEER_STAGE_TOK * nsel, d), lambda i, p: (i * parts + p, 0)),
        ],
        out_specs=pl.BlockSpec((PEER_TOK, d), lambda i, p: (i, 0)),
        out_shape=jax.ShapeDtypeStruct((n_tok, d), F32),
        compiler_params=_params("arbitrary", "arbitrary"),
        name="peer_experts_staged",
    )(gate_t, h2, x2, gate, rows)


def _final_norm_kernel(x_ref, g_ref, o_ref):
    x = x_ref[...]
    ms = jnp.mean(x * x, axis=-1, keepdims=True)
    o_ref[...] = x * lax.rsqrt(ms + NORM_EPS) * g_ref[...]


def _final_norm(x2, g, tm=512):
    t, d = x2.shape
    return pl.pallas_call(
        _final_norm_kernel,
        grid=(t // tm,),
        in_specs=[pl.BlockSpec((tm, d), lambda i: (i, 0)), pl.BlockSpec((1, d), lambda i: (0, 0))],
        out_specs=pl.BlockSpec((tm, d), lambda i: (i, 0)),
        out_shape=jax.ShapeDtypeStruct((t, d), F32),
        compiler_params=_params("arbitrary"),
        name="final_norm",
    )(x2, g.reshape(1, d))


def _pack_expert_rows(u, v):
    hi = lax.bitcast_convert_type(u.astype(BF16), jnp.uint16).astype(jnp.uint32)
    lo = lax.bitcast_convert_type(v.astype(BF16), jnp.uint16).astype(jnp.uint32)
    return lax.bitcast_convert_type((hi << 16) | lo, I32)


def _reorder_in_cols(w):
    n_qkv = SWA_WIDTH + 2 * SWA_KV_WIDTH + 3 * SB_WIDTH
    return jnp.concatenate([w[:, n_qkv:], w[:, :n_qkv]], axis=1)


def kernel(x, c, positions, ada_w, ada_b, norm1_g, w_in, swa_sinks, w_branch_swa, w_branch_sb,
           w_out, norm2_g, peer_wq, peer_subkeys, peer_u, peer_v, final_g):
    b, s, d = x.shape
    depth = ada_w.shape[0]
    t = b * s
    mod = _ada_mod(c, ada_w, ada_b)
    pos3 = positions.reshape(b, s, 1)
    inv_freq = jnp.power(ROPE_THETA, -jnp.arange(HEAD_DIM // 2, dtype=F32) * (2.0 / HEAD_DIM))
    invf = jnp.tile(inv_freq, LANES // (HEAD_DIM // 2)).reshape(1, LANES)
    x2 = x.reshape(t, d)
    n_units = t // PEER_ROUTE_TOK
    tc_units = max(1, int(n_units * PEER_TC_SHARE))
    bounds = [(tc_units + ((n_units - tc_units) * k) // PEER_SC_CHUNKS) * PEER_ROUTE_TOK
              for k in range(PEER_SC_CHUNKS + 1)]
    for l in range(depth):
        m = mod[l].reshape(b, N_MOD, 1, d)
        shift1, scale1, gate1, shift2, scale2, gate2 = [m[:, i] for i in range(N_MOD)]
        w_in_l = _reorder_in_cols(w_in[l]).astype(BF16)
        proj = _norm_proj(x2, norm1_g[l], scale1, shift1, w_in_l, s)
        proj3 = proj.reshape(b, s, IN_COLS)
        y_a = _swa(proj3, pos3, invf, swa_sinks[l])
        y_b = _stick_breaking(proj3)
        x2 = _merge(y_a.reshape(t, SWA_WIDTH), y_b.reshape(t, SB_WIDTH), proj, x2, gate1,
                    w_branch_swa[l].astype(BF16), w_branch_sb[l].astype(BF16),
                    w_out[l].astype(BF16), s)
        sk = peer_subkeys[l].reshape(PEER_HEADS * 2, PEER_NKEYS, PEER_HALF).astype(BF16)
        wq = peer_wq[l].astype(BF16)
        table = _pack_expert_rows(peer_u[l], peer_v[l])
        nsel = PEER_HEADS * PEER_TOPK
        route = lambda start, n_tok: _peer_route(x2, norm2_g[l], scale2, shift2, wq, sk, s, start, n_tok)
        sc_ranges = []
        for lo, hi in zip(bounds[:-1], bounds[1:]):
            h_r, idx_r, gate_r = route(lo, hi - lo)
            staged = _sc_gather_rows(table, idx_r.T.reshape(-1), 0, (hi - lo) * nsel)
            sc_ranges.append((staged, gate_r, h_r, lo, hi - lo))
        h_tc, idx_tc, gate_tc = route(0, bounds[0])
        parts = [_peer_experts(idx_tc.T.reshape(-1), gate_tc, h_tc, x2, gate2, table[:, None, :], s,
                               bounds[0])]
        for staged, gate_r, h_r, start, n_tok in sc_ranges:
            parts.append(_peer_experts_staged(staged, gate_r, h_r, x2, gate2, s, start, n_tok))
        x2 = jnp.concatenate(parts, axis=0)
    return _final_norm(x2, final_g).reshape(b, s, d)
```

```python
import functools

import jax
import jax.numpy as jnp
import numpy as np
from jax import lax
from jax.experimental import pallas as pl
from jax.experimental.pallas import tpu as pltpu
from jax.experimental.pallas import tpu_sc as plsc

F32 = jnp.float32
BF16 = jnp.bfloat16
I32 = jnp.int32

HEAD_DIM = 64
BLOCK = 128
SWA_Q_HEADS = 8
SWA_KV_HEADS = 2
SB_HEADS = 8
ROPE_THETA = 10000.0
SWA_WIDTH = SWA_Q_HEADS * HEAD_DIM
SWA_KV_WIDTH = SWA_KV_HEADS * HEAD_DIM
SB_WIDTH = SB_HEADS * HEAD_DIM
PEER_HEADS = 8
PEER_NKEYS = 128
PEER_TOPK = 16
PEER_QDIM = 256
PEER_HALF = PEER_QDIM // 2
N_MOD = 6
NORM_EPS = 1e-6
NEG_INF = -1e30

LANES = 128
SUBLANES = 8
VMEM_LIMIT_BYTES = 56 * 1024 * 1024

COL_GATE_SWA = 0
COL_GATE_SB = 1024
COL_Q_SWA = 2048
COL_K_SWA = COL_Q_SWA + SWA_WIDTH
COL_V_SWA = COL_K_SWA + SWA_KV_WIDTH
COL_Q_SB = COL_V_SWA + SWA_KV_WIDTH
COL_K_SB = COL_Q_SB + SB_WIDTH
COL_V_SB = COL_K_SB + SB_WIDTH
IN_COLS = COL_V_SB + SB_WIDTH

SB_CHUNK = 2 * BLOCK
PEER_SUB = 4
PEER_TOK = 128
PEER_STAGE_TOK = 32
PEER_TC_SHARE = 0.17
PEER_SC_CHUNKS = 2
PEER_GROUPS = 2
PEER_ROUTE_TOK = 256

SC_CORES = 2
SC_SUBCORES = 16
SC_WINDOW = 64


def _dot(a, b):
    return jnp.dot(a, b, preferred_element_type=F32)


def _dot_nt(a, b):
    return lax.dot_general(a, b, (((1,), (1,)), ((), ())), preferred_element_type=F32)


def _split_bf16(a):
    hi = a.astype(BF16)
    lo = (a - hi.astype(F32)).astype(BF16)
    return hi, lo


def _params(*sem):
    return pltpu.CompilerParams(dimension_semantics=sem, vmem_limit_bytes=VMEM_LIMIT_BYTES)


def _resident(shape, index_map):
    return pl.BlockSpec(shape, index_map, pipeline_mode=pl.Buffered(1))


def _ada_kernel(c_ref, w_ref, b_ref, o_ref):
    c = c_ref[...]
    a = c * jax.nn.sigmoid(c)
    a_hi, a_lo = _split_bf16(a)
    w_hi, w_lo = _split_bf16(w_ref[0])
    acc = _dot(a_hi, w_hi) + _dot(a_hi, w_lo) + _dot(a_lo, w_hi)
    o_ref[0] = acc + b_ref[0]


def _ada_mod(c, ada_w, ada_b):
    depth, d, n = ada_w.shape
    b = c.shape[0]
    tn = n // 4
    return pl.pallas_call(
        _ada_kernel,
        grid=(depth, n // tn),
        in_specs=[
            pl.BlockSpec((b, d), lambda l, j: (0, 0)),
            pl.BlockSpec((1, d, tn), lambda l, j: (l, 0, j)),
            pl.BlockSpec((1, 1, tn), lambda l, j: (l, 0, j)),
        ],
        out_specs=pl.BlockSpec((1, b, tn), lambda l, j: (l, 0, j)),
        out_shape=jax.ShapeDtypeStruct((depth, b, n), F32),
        compiler_params=_params("arbitrary", "arbitrary"),
        name="ada_mod",
    )(c, ada_w, ada_b.reshape(depth, 1, n))


def _modulated_norm(x, g, scale, shift):
    ms = jnp.mean(x * x, axis=-1, keepdims=True)
    y = x * lax.rsqrt(ms + NORM_EPS) * g
    return y * (1.0 + scale) + shift


def _norm_proj_kernel(x_ref, g_ref, scale_ref, shift_ref, w_ref, o_ref, *, col_chunk):
    h = _modulated_norm(x_ref[...], g_ref[...], scale_ref[0], shift_ref[0]).astype(BF16)
    n = o_ref.shape[1]
    for c0 in range(0, n, col_chunk):
        c1 = min(c0 + col_chunk, n)
        o_ref[:, c0:c1] = _dot(h, w_ref[:, c0:c1]).astype(o_ref.dtype)


def _norm_proj(x2, g, scale, shift, w_bf16, seq, tm=512):
    t, d = x2.shape
    n = w_bf16.shape[1]
    per_batch = lambda i: ((i * tm) // seq, 0, 0)
    return pl.pallas_call(
        functools.partial(_norm_proj_kernel, col_chunk=1024),
        grid=(t // tm,),
        in_specs=[
            pl.BlockSpec((tm, d), lambda i: (i, 0)),
            pl.BlockSpec((1, d), lambda i: (0, 0)),
            pl.BlockSpec((1, 1, d), per_batch),
            pl.BlockSpec((1, 1, d), per_batch),
            _resident((d, n), lambda i: (0, 0)),
        ],
        out_specs=pl.BlockSpec((tm, n), lambda i: (i, 0)),
        out_shape=jax.ShapeDtypeStruct((t, n), BF16),
        compiler_params=_params("arbitrary"),
        name="norm_proj",
    )(x2, g.reshape(1, d), scale, shift, w_bf16)


def _swa_kernel(sink_ref, q_ref, kc_ref, kp_ref, vc_ref, vp_ref, pc_ref, pp_ref, invf_ref, o_ref):
    n = pl.program_id(1)
    lane = lax.broadcasted_iota(I32, (BLOCK, LANES), 1)
    first_half = (lane % HEAD_DIM) < (HEAD_DIM // 2)
    low_head = lane < HEAD_DIM
    invf = invf_ref[...]

    def rope_tables(pos_ref):
        ang = pos_ref[0].astype(F32) * invf
        s = jnp.sin(ang)
        return jnp.cos(ang), jnp.where(first_half, -s, s)

    def rope(x, tables):
        c, s_signed = tables
        swapped = jnp.where(first_half, pltpu.roll(x, LANES - HEAD_DIM // 2, 1),
                            pltpu.roll(x, HEAD_DIM // 2, 1))
        return x * c + swapped * s_signed

    tab_c = rope_tables(pc_ref)
    tab_p = rope_tables(pp_ref)
    k = jnp.concatenate([rope(kp_ref[0].astype(F32), tab_p), rope(kc_ref[0].astype(F32), tab_c)], axis=0)
    v = jnp.concatenate([vp_ref[0], vc_ref[0]], axis=0).astype(F32)
    k_var = (k.astype(BF16), pltpu.roll(k, HEAD_DIM, 1).astype(BF16))
    v_var = (v.astype(BF16), pltpu.roll(v, HEAD_DIM, 1).astype(BF16))

    row = lax.broadcasted_iota(I32, (BLOCK, 2 * BLOCK), 0)
    col = lax.broadcasted_iota(I32, (BLOCK, 2 * BLOCK), 1)
    diff = row + BLOCK - col
    valid = (diff >= 0) & (diff < BLOCK) & ((col >= BLOCK) | (n > 0))

    group = SWA_Q_HEADS // SWA_KV_HEADS
    for jt in range(SWA_Q_HEADS // 2):
        q_t = rope(q_ref[0, :, jt * LANES:(jt + 1) * LANES].astype(F32), tab_c) * (HEAD_DIM ** -0.5)
        outs = []
        for hh in range(2):
            j = 2 * jt + hh
            g = j // group
            qm = jnp.where(low_head == (hh == 0), q_t, 0.0).astype(BF16)
            s = _dot_nt(qm, k_var[0 if g == hh else 1])
            s = jnp.where(valid, s, NEG_INF)
            sink = sink_ref[j]
            m = jnp.maximum(jnp.max(s, axis=-1, keepdims=True), sink)
            p = jnp.exp(s - m)
            den = jnp.sum(p, axis=-1, keepdims=True) + jnp.exp(sink - m)
            outs.append(_dot(p.astype(BF16), v_var[0 if g == hh else 1]) / den)
        o_ref[0, :, jt * LANES:(jt + 1) * LANES] = jnp.where(low_head, outs[0], outs[1]).astype(o_ref.dtype)


def _swa(proj3, pos3, invf, sinks):
    b, s, _ = proj3.shape
    nb = s // BLOCK
    cur = lambda c: (lambda i, n: (i, n, c))
    prev = lambda c: (lambda i, n: (i, jnp.maximum(n - 1, 0), c))
    return pl.pallas_call(
        _swa_kernel,
        grid=(b, nb),
        in_specs=[
            pl.BlockSpec(memory_space=pltpu.SMEM),
            pl.BlockSpec((1, BLOCK, SWA_WIDTH), cur(COL_Q_SWA // SWA_WIDTH)),
            pl.BlockSpec((1, BLOCK, LANES), cur(COL_K_SWA // LANES)),
            pl.BlockSpec((1, BLOCK, LANES), prev(COL_K_SWA // LANES)),
            pl.BlockSpec((1, BLOCK, LANES), cur(COL_V_SWA // LANES)),
            pl.BlockSpec((1, BLOCK, LANES), prev(COL_V_SWA // LANES)),
            pl.BlockSpec((1, BLOCK, 1), cur(0)),
            pl.BlockSpec((1, BLOCK, 1), prev(0)),
            pl.BlockSpec((1, LANES), lambda i, n: (0, 0)),
        ],
        out_specs=pl.BlockSpec((1, BLOCK, SWA_WIDTH), lambda i, n: (i, n, 0)),
        out_shape=jax.ShapeDtypeStruct((b, s, SWA_WIDTH), BF16),
        compiler_params=_params("arbitrary", "arbitrary"),
        name="swa",
    )(sinks, proj3, proj3, proj3, proj3, proj3, pos3, pos3, invf)


def _sb_kernel(q_ref, k_ref, v_ref, tri_ref, o_ref):
    qb = pl.program_id(2)
    lane = lax.broadcasted_iota(I32, (SB_CHUNK, LANES), 1)
    low_head = lane < HEAD_DIM
    q = q_ref[0].astype(F32) * (HEAD_DIM ** -0.5)
    qms = [jnp.where(low_head == (h == 0), q, 0.0).astype(BF16) for h in range(2)]
    tri = tri_ref[...]
    row = lax.broadcasted_iota(I32, (SB_CHUNK, SB_CHUNK), 0)
    col = lax.broadcasted_iota(I32, (SB_CHUNK, SB_CHUNK), 1)
    strict = col < row

    def sweep_pair(c0, carry, diagonal):
        c1 = c0 - 1
        exists = c1 >= 0
        off0 = pl.multiple_of(c0 * SB_CHUNK, SB_CHUNK)
        off1 = pl.multiple_of(jnp.maximum(c1, 0) * SB_CHUNK, SB_CHUNK)
        k0 = k_ref[0, pl.ds(off0, SB_CHUNK), :]
        v0 = v_ref[0, pl.ds(off0, SB_CHUNK), :]
        k1 = k_ref[0, pl.ds(off1, SB_CHUNK), :]
        v1 = v_ref[0, pl.ds(off1, SB_CHUNK), :]
        v1 = jnp.where(exists, v1, jnp.zeros_like(v1))
        chains = [(h, kc, vc, diagonal and first) for h in range(2)
                  for kc, vc, first in ((k0, v0, True), (k1, v1, False))]
        zs = [_dot_nt(qms[h], kc) for h, kc, _, _ in chains]
        log_1m = []
        for z, (_, _, _, masked) in zip(zs, chains):
            neg = -z
            t = jnp.minimum(neg, 0.0) - jnp.log(1.0 + jnp.exp(jnp.minimum(z, neg)))
            log_1m.append(jnp.where(strict, t, 0.0) if masked else t)
        sums = [jnp.sum(t, axis=-1, keepdims=True) for t in log_1m]
        after = [_dot(t.astype(BF16), tri) for t in log_1m]
        new = []
        for h in range(2):
            acc, later = carry[2 * h], carry[2 * h + 1]
            for n, later_n in ((2 * h, later), (2 * h + 1, later + sums[2 * h])):
                attn = jnp.exp(zs[n] + log_1m[n] + after[n] + later_n)
                if chains[n][3]:
                    attn = jnp.where(strict, attn, 0.0)
                acc = acc + _dot(attn.astype(BF16), chains[n][2])
            new.append(acc)
            new.append(later + sums[2 * h] + sums[2 * h + 1])
        return tuple(new)

    zeros = (jnp.zeros((SB_CHUNK, LANES), F32), jnp.zeros((SB_CHUNK, 1), F32))
    carry = sweep_pair(qb, zeros + zeros, True)
    n_pairs = lax.shift_right_logical(qb + 2, 1)
    carry = lax.fori_loop(1, n_pairs, lambda i, c: sweep_pair(qb - 2 * i, c, False), carry)
    o_ref[0] = jnp.where(low_head, carry[0], carry[2]).astype(o_ref.dtype)


def _stick_breaking(proj3):
    b, s, _ = proj3.shape
    assert s % SB_CHUNK == 0
    nb = s // SB_CHUNK
    pairs = SB_HEADS // 2
    tri = (np.arange(SB_CHUNK)[:, None] > np.arange(SB_CHUNK)[None, :])
    tri = jnp.asarray(tri, dtype=BF16)
    return pl.pallas_call(
        _sb_kernel,
        grid=(b, pairs, nb),
        in_specs=[
            pl.BlockSpec((1, SB_CHUNK, LANES), lambda i, p, n: (i, n, COL_Q_SB // LANES + p)),
            pl.BlockSpec((1, s, LANES), lambda i, p, n: (i, 0, COL_K_SB // LANES + p)),
            pl.BlockSpec((1, s, LANES), lambda i, p, n: (i, 0, COL_V_SB // LANES + p)),
            pl.BlockSpec((SB_CHUNK, SB_CHUNK), lambda i, p, n: (0, 0)),
        ],
        out_specs=pl.BlockSpec((1, SB_CHUNK, LANES), lambda i, p, n: (i, n, p)),
        out_shape=jax.ShapeDtypeStruct((b, s, SB_WIDTH), BF16),
        compiler_params=_params("arbitrary", "arbitrary", "arbitrary"),
        name="stick_breaking",
    )(proj3, proj3, proj3, tri)


def _merge_kernel(ya_ref, yb_ref, ga_ref, gb_ref, x_ref, gate_ref, wa_ref, wb_ref, wo_ref, o_ref):
    ma = _dot(ya_ref[...], wa_ref[...])
    mb = _dot(yb_ref[...], wb_ref[...])
    merged = (jax.nn.sigmoid(ga_ref[...].astype(F32)) * ma
              + jax.nn.sigmoid(gb_ref[...].astype(F32)) * mb)
    o_ref[...] = x_ref[...] + gate_ref[0] * _dot(merged.astype(BF16), wo_ref[...])


def _merge(ya, yb, proj, x2, gate, wa, wb, wo, seq, tm=512):
    t, d = x2.shape
    per_batch = lambda i: ((i * tm) // seq, 0, 0)
    return pl.pallas_call(
        _merge_kernel,
        grid=(t // tm,),
        in_specs=[
            pl.BlockSpec((tm, SWA_WIDTH), lambda i: (i, 0)),
            pl.BlockSpec((tm, SB_WIDTH), lambda i: (i, 0)),
            pl.BlockSpec((tm, d), lambda i: (i, COL_GATE_SWA // d)),
            pl.BlockSpec((tm, d), lambda i: (i, COL_GATE_SB // d)),
            pl.BlockSpec((tm, d), lambda i: (i, 0)),
            pl.BlockSpec((1, 1, d), per_batch),
            _resident((SWA_WIDTH, d), lambda i: (0, 0)),
            _resident((SB_WIDTH, d), lambda i: (0, 0)),
            _resident((d, d), lambda i: (0, 0)),
        ],
        out_specs=pl.BlockSpec((tm, d), lambda i: (i, 0)),
        out_shape=jax.ShapeDtypeStruct((t, d), F32),
        compiler_params=_params("arbitrary"),
        name="merge_out",
    )(ya, yb, proj, proj, x2, gate, wa, wb, wo)


def _extract_top(vals, payload, k):
    r = vals.shape[0]
    idx = lax.broadcasted_iota(I32, vals.shape, 0)
    top_v, top_p = [], []
    for _ in range(k):
        m = jnp.max(vals, axis=0, keepdims=True)
        first = jnp.min(jnp.where(vals == m, idx, r), axis=0, keepdims=True)
        hit = idx == first
        top_v.append(m)
        if payload is None:
            top_p.append(first)
        else:
            top_p.append(jnp.sum(jnp.where(hit, payload, 0), axis=0, keepdims=True))
        vals = jnp.where(hit, -jnp.inf, vals)
    return top_v, top_p


def _peer_route_kernel(x_ref, g_ref, scale_ref, shift_ref, wq_ref, sk_ref, h_ref, idx_ref, gate_ref):
    h = _modulated_norm(x_ref[...], g_ref[...], scale_ref[0], shift_ref[0])
    h_ref[...] = h
    q = _dot(h.astype(BF16), wq_ref[...]).astype(BF16)
    idx_rows, gate_rows = [], []
    for head in range(PEER_HEADS):
        halves = []
        for p in range(2):
            c0 = (head * 2 + p) * PEER_HALF
            scores = _dot_nt(sk_ref[head * 2 + p], q[:, c0:c0 + PEER_HALF])
            halves.append(_extract_top(scores, None, PEER_TOPK))
        (s0, i0), (s1, i1) = halves
        s1_all = jnp.concatenate(s1, axis=0)
        i1_all = jnp.concatenate(i1, axis=0)
        cand_s, cand_i = [], []
        for i in range(PEER_TOPK):
            n_j = PEER_TOPK // (i + 1)
            cand_s.append(s0[i] + s1_all[0:n_j])
            cand_i.append(i0[i] * PEER_NKEYS + i1_all[0:n_j])
        n_cand = sum(PEER_TOPK // (i + 1) for i in range(PEER_TOPK))
        pad = -n_cand % SUBLANES
        if pad:
            cand_s.append(jnp.full((pad, s1_all.shape[1]), -jnp.inf, F32))
            cand_i.append(jnp.zeros((pad, s1_all.shape[1]), I32))
        best_s, best_i = _extract_top(jnp.concatenate(cand_s, axis=0),
                                      jnp.concatenate(cand_i, axis=0), PEER_TOPK)
        e = [jnp.exp(s - best_s[0]) for s in best_s]
        den = e[0]
        for t in e[1:]:
            den = den + t
        idx_rows += best_i
        gate_rows += [t / den for t in e]
    idx_ref[...] = jnp.concatenate(idx_rows, axis=0)
    gate_ref[...] = jnp.concatenate(gate_rows, axis=0)


def _peer_route(x2, g, scale, shift, wq, sk, seq, tok_start, n_tok, tm=PEER_ROUTE_TOK):
    d = x2.shape[1]
    nq = wq.shape[1]
    nsel = PEER_HEADS * PEER_TOPK
    first = tok_start // tm
    per_batch = lambda i: (((first + i) * tm) // seq, 0, 0)
    return pl.pallas_call(
        _peer_route_kernel,
        grid=(n_tok // tm,),
        in_specs=[
            pl.BlockSpec((tm, d), lambda i: (first + i, 0)),
            pl.BlockSpec((1, d), lambda i: (0, 0)),
            pl.BlockSpec((1, 1, d), per_batch),
            pl.BlockSpec((1, 1, d), per_batch),
            _resident((d, nq), lambda i: (0, 0)),
            _resident(sk.shape, lambda i: (0, 0, 0)),
        ],
        out_specs=[
            pl.BlockSpec((tm, d), lambda i: (i, 0)),
            pl.BlockSpec((nsel, tm), lambda i: (0, i)),
            pl.BlockSpec((nsel, tm), lambda i: (0, i)),
        ],
        out_shape=[
            jax.ShapeDtypeStruct((n_tok, d), F32),
            jax.ShapeDtypeStruct((nsel, n_tok), I32),
            jax.ShapeDtypeStruct((nsel, n_tok), F32),
        ],
        compiler_params=_params("arbitrary"),
        name="peer_route",
    )(x2, g.reshape(1, d), scale, shift, wq, sk)


def _reduce_token(tok, j, rows, gate_ref, h_ref, x_ref, mod_ref, o_ref):
    nsel = PEER_HEADS * PEER_TOPK
    h_row = h_ref[pl.ds(tok, 1), :]
    words = rows[j * nsel:(j + 1) * nsel, :]
    u_rows = lax.bitcast_convert_type(words & jnp.int32(-65536), F32)
    act = jnp.sum(u_rows * h_row, axis=-1, keepdims=True)
    gate_col = pltpu.roll(gate_ref[...], (PEER_TOK - tok) & (PEER_TOK - 1), 1)[:, 0:1]
    w = gate_col * (0.5 * act * (1.0 + lax.erf(act * np.float32(np.sqrt(0.5)))))
    v_rows = lax.bitcast_convert_type(words << 16, F32)
    out_row = jnp.sum(v_rows * w, axis=0, keepdims=True)
    o_ref[pl.ds(tok, 1), :] = x_ref[pl.ds(tok, 1), :] + mod_ref[0] * out_row


def _peer_expert_kernel(idx_ref, idx_next_ref, gate_ref, h_ref, x_ref, mod_ref, tab_ref, o_ref,
                        rows_a, rows_b, sem):
    i = pl.program_id(0)
    nsel = PEER_HEADS * PEER_TOPK
    n_sub = PEER_TOK // PEER_SUB
    bufs = ((rows_a, sem.at[0]), (rows_b, sem.at[1]))

    def row_copy(src_idx_ref, flat, dst, dst_sem, r):
        e = src_idx_ref[flat]
        return pltpu.make_async_copy(tab_ref.at[e], dst.at[pl.ds(r, 1)], dst_sem)

    def start_token(src_idx_ref, base, j, dst, dst_sem):
        for kq in range(nsel):
            row_copy(src_idx_ref, base + j * nsel + kq, dst, dst_sem, j * nsel + kq).start(priority=kq % 2)

    def wait_batch(dst, dst_sem):
        pltpu.make_async_copy(tab_ref.at[pl.ds(0, PEER_SUB * nsel), 0], dst, dst_sem).wait()

    def batch(s, parity, src_idx_ref, next_base):
        src, src_sem = bufs[parity]
        dst, dst_sem = bufs[1 - parity]
        wait_batch(src, src_sem)
        for j in range(PEER_SUB):
            start_token(src_idx_ref, next_base, j, dst, dst_sem)
            _reduce_token(s * PEER_SUB + j, j, src, gate_ref, h_ref, x_ref, mod_ref, o_ref)

    @pl.when(i == 0)
    def _():
        def first(r, _):
            row_copy(idx_ref, r, rows_a, sem.at[0], r).start()
            return 0
        lax.fori_loop(0, PEER_SUB * nsel, first, 0)

    def pair(p, _):
        s = 2 * p
        batch(s, 0, idx_ref, (s + 1) * PEER_SUB * nsel)
        batch(s + 1, 1, idx_ref, (s + 2) * PEER_SUB * nsel)
        return 0

    lax.fori_loop(0, n_sub // 2 - 1, pair, 0)
    batch(n_sub - 2, 0, idx_ref, (n_sub - 1) * PEER_SUB * nsel)
    batch(n_sub - 1, 1, idx_next_ref, 0)

    @pl.when(i == pl.num_programs(0) - 1)
    def _():
        wait_batch(rows_a, sem.at[0])


def _peer_experts(idx_flat, gate_t, h2, x2, gate, table3, seq, n_tok):
    d = x2.shape[1]
    nsel = PEER_HEADS * PEER_TOPK
    n_steps = n_tok // PEER_TOK
    per_batch = lambda i: ((i * PEER_TOK) // seq, 0, 0)
    idx_block = PEER_TOK * nsel
    return pl.pallas_call(
        _peer_expert_kernel,
        grid=(n_steps,),
        in_specs=[
            pl.BlockSpec((idx_block,), lambda i: (i,), memory_space=pltpu.SMEM),
            pl.BlockSpec((idx_block,), lambda i: (jnp.minimum(i + 1, n_steps - 1),),
                         memory_space=pltpu.SMEM),
            pl.BlockSpec((nsel, PEER_TOK), lambda i: (0, i)),
            pl.BlockSpec((PEER_TOK, d), lambda i: (i, 0)),
            pl.BlockSpec((PEER_TOK, d), lambda i: (i, 0)),
            pl.BlockSpec((1, 1, d), per_batch),
            pl.BlockSpec(memory_space=pl.ANY),
        ],
        out_specs=pl.BlockSpec((PEER_TOK, d), lambda i: (i, 0)),
        out_shape=jax.ShapeDtypeStruct((n_tok, d), F32),
        scratch_shapes=[
            pltpu.VMEM((PEER_SUB * nsel, d), I32),
            pltpu.VMEM((PEER_SUB * nsel, d), I32),
            pltpu.SemaphoreType.DMA((2,)),
        ],
        compiler_params=_params("arbitrary"),
        name="peer_experts",
    )(idx_flat, idx_flat, gate_t, h2, x2, gate, table3)


def _sc_gather_rows(table, idx_flat, row_start, n_rows):
    d = table.shape[1]
    n_workers = SC_CORES * SC_SUBCORES
    assert n_rows % (n_workers * SC_WINDOW) == 0
    per_worker = n_rows // n_workers
    mesh = plsc.VectorSubcoreMesh(core_axis_name="c", subcore_axis_name="s")

    @functools.partial(
        pl.kernel, mesh=mesh,
        out_type=jax.ShapeDtypeStruct((n_rows, d), table.dtype),
        scratch_types=[
            pltpu.VMEM((SC_WINDOW,), I32),
            pltpu.VMEM((SC_WINDOW, d), table.dtype),
            pltpu.SemaphoreType.DMA,
        ],
        name="sc_gather_rows",
    )
    def gather(table_hbm, idx_hbm, out_hbm, idx_v, rows_v, sem):
        base = (lax.axis_index("s") * SC_CORES + lax.axis_index("c")) * per_worker

        @pl.loop(0, per_worker // SC_WINDOW)
        def _(w):
            off = base + w * SC_WINDOW
            pltpu.sync_copy(idx_hbm.at[pl.ds(row_start + off, SC_WINDOW)], idx_v)
            pltpu.async_copy(table_hbm.at[idx_v], rows_v, sem).wait()
            pltpu.sync_copy(rows_v, out_hbm.at[pl.ds(off, SC_WINDOW)])

    return gather(table, idx_flat)


def _peer_staged_kernel(gate_ref, h_ref, x_ref, mod_ref, rows_ref, o_ref):
    part = pl.program_id(1)
    for j in range(PEER_STAGE_TOK):
        _reduce_token(part * PEER_STAGE_TOK + j, j, rows_ref, gate_ref, h_ref, x_ref, mod_ref, o_ref)


def _peer_experts_staged(rows, gate_t, h2, x2, gate, seq, tok_start, n_tok):
    d = x2.shape[1]
    nsel = PEER_HEADS * PEER_TOPK
    first = tok_start // PEER_TOK
    parts = PEER_TOK // PEER_STAGE_TOK
    return pl.pallas_call(
        _peer_staged_kernel,
        grid=(n_tok // PEER_TOK, parts),
        in_specs=[
            pl.BlockSpec((nsel, PEER_TOK), lambda i, p: (0, i)),
            pl.BlockSpec((PEER_TOK, d), lambda i, p: (i, 0)),
            pl.BlockSpec((PEER_TOK, d), lambda i, p: (first + i, 0)),
            pl.BlockSpec((1, 1, d), lambda i, p: (((first + i) * PEER_TOK) // seq, 0, 0)),
            pl.BlockSpec((PEER_STAGE_TOK * nsel, d), lambda i, p: (i * parts + p, 0)),
        ],
        out_specs=pl.BlockSpec((PEER_TOK, d), lambda i, p: (i, 0)),
        out_shape=jax.ShapeDtypeStruct((n_tok, d), F32),
        compiler_params=_params("arbitrary", "arbitrary"),
        name="peer_experts_staged",
    )(gate_t, h2, x2, gate, rows)


def _final_norm_kernel(x_ref, g_ref, o_ref):
    x = x_ref[...]
    ms = jnp.mean(x * x, axis=-1, keepdims=True)
    o_ref[...] = x * lax.rsqrt(ms + NORM_EPS) * g_ref[...]


def _final_norm(x2, g, tm=512):
    t, d = x2.shape
    return pl.pallas_call(
        _final_norm_kernel,
        grid=(t // tm,),
        in_specs=[pl.BlockSpec((tm, d), lambda i: (i, 0)), pl.BlockSpec((1, d), lambda i: (0, 0))],
        out_specs=pl.BlockSpec((tm, d), lambda i: (i, 0)),
        out_shape=jax.ShapeDtypeStruct((t, d), F32),
        compiler_params=_params("arbitrary"),
        name="final_norm",
    )(x2, g.reshape(1, d))


def _pack_expert_rows(u, v):
    hi = lax.bitcast_convert_type(u.astype(BF16), jnp.uint16).astype(jnp.uint32)
    lo = lax.bitcast_convert_type(v.astype(BF16), jnp.uint16).astype(jnp.uint32)
    return lax.bitcast_convert_type((hi << 16) | lo, I32)


def _reorder_in_cols(w):
    n_qkv = SWA_WIDTH + 2 * SWA_KV_WIDTH + 3 * SB_WIDTH
    return jnp.concatenate([w[:, n_qkv:], w[:, :n_qkv]], axis=1)


def kernel(x, c, positions, ada_w, ada_b, norm1_g, w_in, swa_sinks, w_branch_swa, w_branch_sb,
           w_out, norm2_g, peer_wq, peer_subkeys, peer_u, peer_v, final_g):
    b, s, d = x.shape
    depth = ada_w.shape[0]
    nsel = PEER_HEADS * PEER_TOPK
    mod = _ada_mod(c, ada_w, ada_b)
    pos3 = positions.reshape(b, s, 1)
    inv_freq = jnp.power(ROPE_THETA, -jnp.arange(HEAD_DIM // 2, dtype=F32) * (2.0 / HEAD_DIM))
    invf = jnp.tile(inv_freq, LANES // (HEAD_DIM // 2)).reshape(1, LANES)

    n_groups = PEER_GROUPS if b % PEER_GROUPS == 0 else 1
    bg = b // n_groups
    tg = bg * s
    xs = [x[g * bg:(g + 1) * bg].reshape(tg, d) for g in range(n_groups)]
    n_units = tg // PEER_ROUTE_TOK
    tc_units = max(1, int(n_units * PEER_TC_SHARE))
    bounds = [(tc_units + ((n_units - tc_units) * k) // PEER_SC_CHUNKS) * PEER_ROUTE_TOK
              for k in range(PEER_SC_CHUNKS + 1)]

    layers = []
    for l in range(depth):
        table = _pack_expert_rows(peer_u[l], peer_v[l])
        layers.append(dict(
            w_in=_reorder_in_cols(w_in[l]).astype(BF16), wa=w_branch_swa[l].astype(BF16),
            wb=w_branch_sb[l].astype(BF16), wo=w_out[l].astype(BF16), wq=peer_wq[l].astype(BF16),
            sk=peer_subkeys[l].reshape(PEER_HEADS * 2, PEER_NKEYS, PEER_HALF).astype(BF16),
            table=table, table3=table[:, None, :]))

    def modulation(l, g):
        m = mod[l, g * bg:(g + 1) * bg].reshape(bg, N_MOD, 1, d)
        return [m[:, i] for i in range(N_MOD)]

    def attention(l, g):
        w = layers[l]
        shift1, scale1, gate1 = modulation(l, g)[:3]
        proj = _norm_proj(xs[g], norm1_g[l], scale1, shift1, w["w_in"], s)
        proj3 = proj.reshape(bg, s, IN_COLS)
        y_a = _swa(proj3, pos3[g * bg:(g + 1) * bg], invf, swa_sinks[l])
        y_b = _stick_breaking(proj3)
        xs[g] = _merge(y_a.reshape(tg, SWA_WIDTH), y_b.reshape(tg, SB_WIDTH), proj, xs[g], gate1,
                       w["wa"], w["wb"], w["wo"], s)

    def peer_start(l, g):
        w = layers[l]
        _, _, _, shift2, scale2, _ = modulation(l, g)
        route = lambda start, n_tok: _peer_route(xs[g], norm2_g[l], scale2, shift2, w["wq"], w["sk"], s,
                                                 start, n_tok)
        sc_ranges = []
        for lo, hi in zip(bounds[:-1], bounds[1:]):
            h_r, idx_r, gate_r = route(lo, hi - lo)
            staged = _sc_gather_rows(w["table"], idx_r.T.reshape(-1), 0, (hi - lo) * nsel)
            sc_ranges.append((staged, gate_r, h_r, lo, hi - lo))
        return sc_ranges, route(0, bounds[0])

    def peer_finish(l, g, state):
        sc_ranges, (h_tc, idx_tc, gate_tc) = state
        gate2 = modulation(l, g)[5]
        parts = [_peer_experts(idx_tc.T.reshape(-1), gate_tc, h_tc, xs[g], gate2, layers[l]["table3"], s,
                               bounds[0])]
        for staged, gate_r, h_r, start, n_tok in sc_ranges:
            parts.append(_peer_experts_staged(staged, gate_r, h_r, xs[g], gate2, s, start, n_tok))
        xs[g] = jnp.concatenate(parts, axis=0)

    states = {}
    for l in range(depth):
        for g in range(n_groups):
            if l > 0:
                peer_finish(l - 1, g, states.pop((l - 1, g)))
            attention(l, g)
            states[(l, g)] = peer_start(l, g)
    for g in range(n_groups):
        peer_finish(depth - 1, g, states.pop((depth - 1, g)))
    return _final_norm(jnp.concatenate(xs, axis=0), final_g).reshape(b, s, d)
```

```python
import functools

import jax
import jax.numpy as jnp
import numpy as np
from jax import lax
from jax.experimental import pallas as pl
from jax.experimental.pallas import tpu as pltpu
from jax.experimental.pallas import tpu_sc as plsc

F32 = jnp.float32
BF16 = jnp.bfloat16
I32 = jnp.int32

HEAD_DIM = 64
BLOCK = 128
SWA_Q_HEADS = 8
SWA_KV_HEADS = 2
SB_HEADS = 8
ROPE_THETA = 10000.0
SWA_WIDTH = SWA_Q_HEADS * HEAD_DIM
SWA_KV_WIDTH = SWA_KV_HEADS * HEAD_DIM
SB_WIDTH = SB_HEADS * HEAD_DIM
PEER_HEADS = 8
PEER_NKEYS = 128
PEER_TOPK = 16
PEER_QDIM = 256
PEER_HALF = PEER_QDIM // 2
N_MOD = 6
NORM_EPS = 1e-6
NEG_INF = -1e30

LANES = 128
SUBLANES = 8
VMEM_LIMIT_BYTES = 56 * 1024 * 1024

COL_GATE_SWA = 0
COL_GATE_SB = 1024
COL_Q_SWA = 2048
COL_K_SWA = COL_Q_SWA + SWA_WIDTH
COL_V_SWA = COL_K_SWA + SWA_KV_WIDTH
COL_Q_SB = COL_V_SWA + SWA_KV_WIDTH
COL_K_SB = COL_Q_SB + SB_WIDTH
COL_V_SB = COL_K_SB + SB_WIDTH
IN_COLS = COL_V_SB + SB_WIDTH

SB_CHUNK = 2 * BLOCK
PEER_SUB = 4
PEER_TOK = 128
PEER_STAGE_TOK = 16
PEER_STAGE_SLOTS = 3
PEER_TC_SHARE = 0.17
PEER_SC_CHUNKS = 2
PEER_GROUPS = 2
PEER_ROUTE_TOK = 256

SC_CORES = 2
SC_SUBCORES = 16
SC_WINDOW = 64


def _dot(a, b):
    return jnp.dot(a, b, preferred_element_type=F32)


def _dot_nt(a, b):
    return lax.dot_general(a, b, (((1,), (1,)), ((), ())), preferred_element_type=F32)


def _split_bf16(a):
    hi = a.astype(BF16)
    lo = (a - hi.astype(F32)).astype(BF16)
    return hi, lo


def _params(*sem):
    return pltpu.CompilerParams(dimension_semantics=sem, vmem_limit_bytes=VMEM_LIMIT_BYTES)


def _resident(shape, index_map):
    return pl.BlockSpec(shape, index_map, pipeline_mode=pl.Buffered(1))


def _ada_kernel(c_ref, w_ref, b_ref, o_ref):
    c = c_ref[...]
    a = c * jax.nn.sigmoid(c)
    a_hi, a_lo = _split_bf16(a)
    w_hi, w_lo = _split_bf16(w_ref[0])
    acc = _dot(a_hi, w_hi) + _dot(a_hi, w_lo) + _dot(a_lo, w_hi)
    o_ref[0] = acc + b_ref[0]


def _ada_mod(c, ada_w, ada_b):
    depth, d, n = ada_w.shape
    b = c.shape[0]
    tn = n // 4
    return pl.pallas_call(
        _ada_kernel,
        grid=(depth, n // tn),
        in_specs=[
            pl.BlockSpec((b, d), lambda l, j: (0, 0)),
            pl.BlockSpec((1, d, tn), lambda l, j: (l, 0, j)),
            pl.BlockSpec((1, 1, tn), lambda l, j: (l, 0, j)),
        ],
        out_specs=pl.BlockSpec((1, b, tn), lambda l, j: (l, 0, j)),
        out_shape=jax.ShapeDtypeStruct((depth, b, n), F32),
        compiler_params=_params("arbitrary", "arbitrary"),
        name="ada_mod",
    )(c, ada_w, ada_b.reshape(depth, 1, n))


def _modulated_norm(x, g, scale, shift):
    ms = jnp.mean(x * x, axis=-1, keepdims=True)
    y = x * lax.rsqrt(ms + NORM_EPS) * g
    return y * (1.0 + scale) + shift


def _norm_proj_kernel(x_ref, g_ref, scale_ref, shift_ref, w_ref, o_ref, *, col_chunk):
    h = _modulated_norm(x_ref[...], g_ref[...], scale_ref[0], shift_ref[0]).astype(BF16)
    n = o_ref.shape[1]
    for c0 in range(0, n, col_chunk):
        c1 = min(c0 + col_chunk, n)
        o_ref[:, c0:c1] = _dot(h, w_ref[:, c0:c1]).astype(o_ref.dtype)


def _norm_proj(x2, g, scale, shift, w_bf16, seq, tm=512):
    t, d = x2.shape
    n = w_bf16.shape[1]
    per_batch = lambda i: ((i * tm) // seq, 0, 0)
    return pl.pallas_call(
        functools.partial(_norm_proj_kernel, col_chunk=1024),
        grid=(t // tm,),
        in_specs=[
            pl.BlockSpec((tm, d), lambda i: (i, 0)),
            pl.BlockSpec((1, d), lambda i: (0, 0)),
            pl.BlockSpec((1, 1, d), per_batch),
            pl.BlockSpec((1, 1, d), per_batch),
            _resident((d, n), lambda i: (0, 0)),
        ],
        out_specs=pl.BlockSpec((tm, n), lambda i: (i, 0)),
        out_shape=jax.ShapeDtypeStruct((t, n), BF16),
        compiler_params=_params("arbitrary"),
        name="norm_proj",
    )(x2, g.reshape(1, d), scale, shift, w_bf16)


def _swa_kernel(sink_ref, q_ref, kc_ref, kp_ref, vc_ref, vp_ref, pc_ref, pp_ref, invf_ref, o_ref):
    n = pl.program_id(1)
    lane = lax.broadcasted_iota(I32, (BLOCK, LANES), 1)
    first_half = (lane % HEAD_DIM) < (HEAD_DIM // 2)
    low_head = lane < HEAD_DIM
    invf = invf_ref[...]

    def rope_tables(pos_ref):
        ang = pos_ref[0].astype(F32) * invf
        s = jnp.sin(ang)
        return jnp.cos(ang), jnp.where(first_half, -s, s)

    def rope(x, tables):
        c, s_signed = tables
        swapped = jnp.where(first_half, pltpu.roll(x, LANES - HEAD_DIM // 2, 1),
                            pltpu.roll(x, HEAD_DIM // 2, 1))
        return x * c + swapped * s_signed

    tab_c = rope_tables(pc_ref)
    tab_p = rope_tables(pp_ref)
    k = jnp.concatenate([rope(kp_ref[0].astype(F32), tab_p), rope(kc_ref[0].astype(F32), tab_c)], axis=0)
    v = jnp.concatenate([vp_ref[0], vc_ref[0]], axis=0).astype(F32)
    k_var = (k.astype(BF16), pltpu.roll(k, HEAD_DIM, 1).astype(BF16))
    v_var = (v.astype(BF16), pltpu.roll(v, HEAD_DIM, 1).astype(BF16))

    row = lax.broadcasted_iota(I32, (BLOCK, 2 * BLOCK), 0)
    col = lax.broadcasted_iota(I32, (BLOCK, 2 * BLOCK), 1)
    diff = row + BLOCK - col
    valid = (diff >= 0) & (diff < BLOCK) & ((col >= BLOCK) | (n > 0))

    group = SWA_Q_HEADS // SWA_KV_HEADS
    for jt in range(SWA_Q_HEADS // 2):
        q_t = rope(q_ref[0, :, jt * LANES:(jt + 1) * LANES].astype(F32), tab_c) * (HEAD_DIM ** -0.5)
        outs = []
        for hh in range(2):
            j = 2 * jt + hh
            g = j // group
            qm = jnp.where(low_head == (hh == 0), q_t, 0.0).astype(BF16)
            s = _dot_nt(qm, k_var[0 if g == hh else 1])
            s = jnp.where(valid, s, NEG_INF)
            sink = sink_ref[j]
            m = jnp.maximum(jnp.max(s, axis=-1, keepdims=True), sink)
            p = jnp.exp(s - m)
            den = jnp.sum(p, axis=-1, keepdims=True) + jnp.exp(sink - m)
            outs.append(_dot(p.astype(BF16), v_var[0 if g == hh else 1]) / den)
        o_ref[0, :, jt * LANES:(jt + 1) * LANES] = jnp.where(low_head, outs[0], outs[1]).astype(o_ref.dtype)


def _swa(proj3, pos3, invf, sinks):
    b, s, _ = proj3.shape
    nb = s // BLOCK
    cur = lambda c: (lambda i, n: (i, n, c))
    prev = lambda c: (lambda i, n: (i, jnp.maximum(n - 1, 0), c))
    return pl.pallas_call(
        _swa_kernel,
        grid=(b, nb),
        in_specs=[
            pl.BlockSpec(memory_space=pltpu.SMEM),
            pl.BlockSpec((1, BLOCK, SWA_WIDTH), cur(COL_Q_SWA // SWA_WIDTH)),
            pl.BlockSpec((1, BLOCK, LANES), cur(COL_K_SWA // LANES)),
            pl.BlockSpec((1, BLOCK, LANES), prev(COL_K_SWA // LANES)),
            pl.BlockSpec((1, BLOCK, LANES), cur(COL_V_SWA // LANES)),
            pl.BlockSpec((1, BLOCK, LANES), prev(COL_V_SWA // LANES)),
            pl.BlockSpec((1, BLOCK, 1), cur(0)),
            pl.BlockSpec((1, BLOCK, 1), prev(0)),
            pl.BlockSpec((1, LANES), lambda i, n: (0, 0)),
        ],
        out_specs=pl.BlockSpec((1, BLOCK, SWA_WIDTH), lambda i, n: (i, n, 0)),
        out_shape=jax.ShapeDtypeStruct((b, s, SWA_WIDTH), BF16),
        compiler_params=_params("arbitrary", "arbitrary"),
        name="swa",
    )(sinks, proj3, proj3, proj3, proj3, proj3, pos3, pos3, invf)


def _sb_kernel(q_ref, k_ref, v_ref, tri_ref, o_ref):
    qb = pl.program_id(2)
    lane = lax.broadcasted_iota(I32, (SB_CHUNK, LANES), 1)
    low_head = lane < HEAD_DIM
    q = q_ref[0].astype(F32) * (HEAD_DIM ** -0.5)
    qms = [jnp.where(low_head == (h == 0), q, 0.0).astype(BF16) for h in range(2)]
    tri = tri_ref[...]
    row = lax.broadcasted_iota(I32, (SB_CHUNK, SB_CHUNK), 0)
    col = lax.broadcasted_iota(I32, (SB_CHUNK, SB_CHUNK), 1)
    strict = col < row

    def sweep_pair(c0, carry, diagonal):
        c1 = c0 - 1
        exists = c1 >= 0
        off0 = pl.multiple_of(c0 * SB_CHUNK, SB_CHUNK)
        off1 = pl.multiple_of(jnp.maximum(c1, 0) * SB_CHUNK, SB_CHUNK)
        k0 = k_ref[0, pl.ds(off0, SB_CHUNK), :]
        v0 = v_ref[0, pl.ds(off0, SB_CHUNK), :]
        k1 = k_ref[0, pl.ds(off1, SB_CHUNK), :]
        v1 = v_ref[0, pl.ds(off1, SB_CHUNK), :]
        v1 = jnp.where(exists, v1, jnp.zeros_like(v1))
        chains = [(h, kc, vc, diagonal and first) for h in range(2)
                  for kc, vc, first in ((k0, v0, True), (k1, v1, False))]
        zs = [_dot_nt(qms[h], kc) for h, kc, _, _ in chains]
        log_1m = []
        for z, (_, _, _, masked) in zip(zs, chains):
            neg = -z
            t = jnp.minimum(neg, 0.0) - jnp.log(1.0 + jnp.exp(jnp.minimum(z, neg)))
            log_1m.append(jnp.where(strict, t, 0.0) if masked else t)
        sums = [jnp.sum(t, axis=-1, keepdims=True) for t in log_1m]
        after = [_dot(t.astype(BF16), tri) for t in log_1m]
        new = []
        for h in range(2):
            acc, later = carry[2 * h], carry[2 * h + 1]
            for n, later_n in ((2 * h, later), (2 * h + 1, later + sums[2 * h])):
                attn = jnp.exp(zs[n] + log_1m[n] + after[n] + later_n)
                if chains[n][3]:
                    attn = jnp.where(strict, attn, 0.0)
                acc = acc + _dot(attn.astype(BF16), chains[n][2])
            new.append(acc)
            new.append(later + sums[2 * h] + sums[2 * h + 1])
        return tuple(new)

    zeros = (jnp.zeros((SB_CHUNK, LANES), F32), jnp.zeros((SB_CHUNK, 1), F32))
    carry = sweep_pair(qb, zeros + zeros, True)
    n_pairs = lax.shift_right_logical(qb + 2, 1)
    carry = lax.fori_loop(1, n_pairs, lambda i, c: sweep_pair(qb - 2 * i, c, False), carry)
    o_ref[0] = jnp.where(low_head, carry[0], carry[2]).astype(o_ref.dtype)


def _stick_breaking(proj3):
    b, s, _ = proj3.shape
    assert s % SB_CHUNK == 0
    nb = s // SB_CHUNK
    pairs = SB_HEADS // 2
    tri = (np.arange(SB_CHUNK)[:, None] > np.arange(SB_CHUNK)[None, :])
    tri = jnp.asarray(tri, dtype=BF16)
    return pl.pallas_call(
        _sb_kernel,
        grid=(b, pairs, nb),
        in_specs=[
            pl.BlockSpec((1, SB_CHUNK, LANES), lambda i, p, n: (i, n, COL_Q_SB // LANES + p)),
            pl.BlockSpec((1, s, LANES), lambda i, p, n: (i, 0, COL_K_SB // LANES + p)),
            pl.BlockSpec((1, s, LANES), lambda i, p, n: (i, 0, COL_V_SB // LANES + p)),
            pl.BlockSpec((SB_CHUNK, SB_CHUNK), lambda i, p, n: (0, 0)),
        ],
        out_specs=pl.BlockSpec((1, SB_CHUNK, LANES), lambda i, p, n: (i, n, p)),
        out_shape=jax.ShapeDtypeStruct((b, s, SB_WIDTH), BF16),
        compiler_params=_params("arbitrary", "arbitrary", "arbitrary"),
        name="stick_breaking",
    )(proj3, proj3, proj3, tri)


def _merge_kernel(ya_ref, yb_ref, ga_ref, gb_ref, x_ref, gate_ref, wa_ref, wb_ref, wo_ref, o_ref):
    ma = _dot(ya_ref[...], wa_ref[...])
    mb = _dot(yb_ref[...], wb_ref[...])
    merged = (jax.nn.sigmoid(ga_ref[...].astype(F32)) * ma
              + jax.nn.sigmoid(gb_ref[...].astype(F32)) * mb)
    o_ref[...] = x_ref[...] + gate_ref[0] * _dot(merged.astype(BF16), wo_ref[...])


def _merge(ya, yb, proj, x2, gate, wa, wb, wo, seq, tm=512):
    t, d = x2.shape
    per_batch = lambda i: ((i * tm) // seq, 0, 0)
    return pl.pallas_call(
        _merge_kernel,
        grid=(t // tm,),
        in_specs=[
            pl.BlockSpec((tm, SWA_WIDTH), lambda i: (i, 0)),
            pl.BlockSpec((tm, SB_WIDTH), lambda i: (i, 0)),
            pl.BlockSpec((tm, d), lambda i: (i, COL_GATE_SWA // d)),
            pl.BlockSpec((tm, d), lambda i: (i, COL_GATE_SB // d)),
            pl.BlockSpec((tm, d), lambda i: (i, 0)),
            pl.BlockSpec((1, 1, d), per_batch),
            _resident((SWA_WIDTH, d), lambda i: (0, 0)),
            _resident((SB_WIDTH, d), lambda i: (0, 0)),
            _resident((d, d), lambda i: (0, 0)),
        ],
        out_specs=pl.BlockSpec((tm, d), lambda i: (i, 0)),
        out_shape=jax.ShapeDtypeStruct((t, d), F32),
        compiler_params=_params("arbitrary"),
        name="merge_out",
    )(ya, yb, proj, proj, x2, gate, wa, wb, wo)


def _extract_top(vals, payload, k):
    r = vals.shape[0]
    idx = lax.broadcasted_iota(I32, vals.shape, 0)
    top_v, top_p = [], []
    for _ in range(k):
        m = jnp.max(vals, axis=0, keepdims=True)
        first = jnp.min(jnp.where(vals == m, idx, r), axis=0, keepdims=True)
        hit = idx == first
        top_v.append(m)
        if payload is None:
            top_p.append(first)
        else:
            top_p.append(jnp.sum(jnp.where(hit, payload, 0), axis=0, keepdims=True))
        vals = jnp.where(hit, -jnp.inf, vals)
    return top_v, top_p


def _peer_route_kernel(x_ref, g_ref, scale_ref, shift_ref, wq_ref, sk_ref, h_ref, idx_ref, gate_ref):
    h = _modulated_norm(x_ref[...], g_ref[...], scale_ref[0], shift_ref[0])
    h_ref[...] = h
    q = _dot(h.astype(BF16), wq_ref[...]).astype(BF16)
    idx_rows, gate_rows = [], []
    for head in range(PEER_HEADS):
        halves = []
        for p in range(2):
            c0 = (head * 2 + p) * PEER_HALF
            scores = _dot_nt(sk_ref[head * 2 + p], q[:, c0:c0 + PEER_HALF])
            halves.append(_extract_top(scores, None, PEER_TOPK))
        (s0, i0), (s1, i1) = halves
        s1_all = jnp.concatenate(s1, axis=0)
        i1_all = jnp.concatenate(i1, axis=0)
        cand_s, cand_i = [], []
        for i in range(PEER_TOPK):
            n_j = PEER_TOPK // (i + 1)
            cand_s.append(s0[i] + s1_all[0:n_j])
            cand_i.append(i0[i] * PEER_NKEYS + i1_all[0:n_j])
        n_cand = sum(PEER_TOPK // (i + 1) for i in range(PEER_TOPK))
        pad = -n_cand % SUBLANES
        if pad:
            cand_s.append(jnp.full((pad, s1_all.shape[1]), -jnp.inf, F32))
            cand_i.append(jnp.zeros((pad, s1_all.shape[1]), I32))
        best_s, best_i = _extract_top(jnp.concatenate(cand_s, axis=0),
                                      jnp.concatenate(cand_i, axis=0), PEER_TOPK)
        e = [jnp.exp(s - best_s[0]) for s in best_s]
        den = e[0]
        for t in e[1:]:
            den = den + t
        idx_rows += best_i
        gate_rows += [t / den for t in e]
    idx_ref[...] = jnp.concatenate(idx_rows, axis=0)
    gate_ref[...] = jnp.concatenate(gate_rows, axis=0)


def _peer_route(x2, g, scale, shift, wq, sk, seq, tok_start, n_tok, tm=PEER_ROUTE_TOK):
    d = x2.shape[1]
    nq = wq.shape[1]
    nsel = PEER_HEADS * PEER_TOPK
    first = tok_start // tm
    per_batch = lambda i: (((first + i) * tm) // seq, 0, 0)
    return pl.pallas_call(
        _peer_route_kernel,
        grid=(n_tok // tm,),
        in_specs=[
            pl.BlockSpec((tm, d), lambda i: (first + i, 0)),
            pl.BlockSpec((1, d), lambda i: (0, 0)),
            pl.BlockSpec((1, 1, d), per_batch),
            pl.BlockSpec((1, 1, d), per_batch),
            _resident((d, nq), lambda i: (0, 0)),
            _resident(sk.shape, lambda i: (0, 0, 0)),
        ],
        out_specs=[
            pl.BlockSpec((tm, d), lambda i: (i, 0)),
            pl.BlockSpec((nsel, tm), lambda i: (0, i)),
            pl.BlockSpec((nsel, tm), lambda i: (0, i)),
        ],
        out_shape=[
            jax.ShapeDtypeStruct((n_tok, d), F32),
            jax.ShapeDtypeStruct((nsel, n_tok), I32),
            jax.ShapeDtypeStruct((nsel, n_tok), F32),
        ],
        compiler_params=_params("arbitrary"),
        name="peer_route",
    )(x2, g.reshape(1, d), scale, shift, wq, sk)


def _reduce_token(tok, j, rows, gate_ref, h_ref, x_ref, mod_ref, o_ref):
    nsel = PEER_HEADS * PEER_TOPK
    h_row = h_ref[pl.ds(tok, 1), :]
    words = rows[j * nsel:(j + 1) * nsel, :]
    u_rows = lax.bitcast_convert_type(words & jnp.int32(-65536), F32)
    act = jnp.sum(u_rows * h_row, axis=-1, keepdims=True)
    gate_col = pltpu.roll(gate_ref[...], (PEER_TOK - tok) & (PEER_TOK - 1), 1)[:, 0:1]
    w = gate_col * (0.5 * act * (1.0 + lax.erf(act * np.float32(np.sqrt(0.5)))))
    v_rows = lax.bitcast_convert_type(words << 16, F32)
    out_row = jnp.sum(v_rows * w, axis=0, keepdims=True)
    o_ref[pl.ds(tok, 1), :] = x_ref[pl.ds(tok, 1), :] + mod_ref[0] * out_row


def _peer_expert_kernel(idx_ref, idx_next_ref, gate_ref, h_ref, x_ref, mod_ref, tab_ref, o_ref,
                        rows_a, rows_b, sem):
    i = pl.program_id(0)
    nsel = PEER_HEADS * PEER_TOPK
    n_sub = PEER_TOK // PEER_SUB
    bufs = ((rows_a, sem.at[0]), (rows_b, sem.at[1]))

    def row_copy(src_idx_ref, flat, dst, dst_sem, r):
        e = src_idx_ref[flat]
        return pltpu.make_async_copy(tab_ref.at[e], dst.at[pl.ds(r, 1)], dst_sem)

    def start_token(src_idx_ref, base, j, dst, dst_sem):
        for kq in range(nsel):
            row_copy(src_idx_ref, base + j * nsel + kq, dst, dst_sem, j * nsel + kq).start(priority=kq % 2)

    def wait_batch(dst, dst_sem):
        pltpu.make_async_copy(tab_ref.at[pl.ds(0, PEER_SUB * nsel), 0], dst, dst_sem).wait()

    def batch(s, parity, src_idx_ref, next_base):
        src, src_sem = bufs[parity]
        dst, dst_sem = bufs[1 - parity]
        wait_batch(src, src_sem)
        for j in range(PEER_SUB):
            start_token(src_idx_ref, next_base, j, dst, dst_sem)
            _reduce_token(s * PEER_SUB + j, j, src, gate_ref, h_ref, x_ref, mod_ref, o_ref)

    @pl.when(i == 0)
    def _():
        def first(r, _):
            row_copy(idx_ref, r, rows_a, sem.at[0], r).start()
            return 0
        lax.fori_loop(0, PEER_SUB * nsel, first, 0)

    def pair(p, _):
        s = 2 * p
        batch(s, 0, idx_ref, (s + 1) * PEER_SUB * nsel)
        batch(s + 1, 1, idx_ref, (s + 2) * PEER_SUB * nsel)
        return 0

    lax.fori_loop(0, n_sub // 2 - 1, pair, 0)
    batch(n_sub - 2, 0, idx_ref, (n_sub - 1) * PEER_SUB * nsel)
    batch(n_sub - 1, 1, idx_next_ref, 0)

    @pl.when(i == pl.num_programs(0) - 1)
    def _():
        wait_batch(rows_a, sem.at[0])


def _peer_experts(idx_flat, gate_t, h2, x2, gate, table3, seq, n_tok):
    d = x2.shape[1]
    nsel = PEER_HEADS * PEER_TOPK
    n_steps = n_tok // PEER_TOK
    per_batch = lambda i: ((i * PEER_TOK) // seq, 0, 0)
    idx_block = PEER_TOK * nsel
    return pl.pallas_call(
        _peer_expert_kernel,
        grid=(n_steps,),
        in_specs=[
            pl.BlockSpec((idx_block,), lambda i: (i,), memory_space=pltpu.SMEM),
            pl.BlockSpec((idx_block,), lambda i: (jnp.minimum(i + 1, n_steps - 1),),
                         memory_space=pltpu.SMEM),
            pl.BlockSpec((nsel, PEER_TOK), lambda i: (0, i)),
            pl.BlockSpec((PEER_TOK, d), lambda i: (i, 0)),
            pl.BlockSpec((PEER_TOK, d), lambda i: (i, 0)),
            pl.BlockSpec((1, 1, d), per_batch),
            pl.BlockSpec(memory_space=pl.ANY),
        ],
        out_specs=pl.BlockSpec((PEER_TOK, d), lambda i: (i, 0)),
        out_shape=jax.ShapeDtypeStruct((n_tok, d), F32),
        scratch_shapes=[
            pltpu.VMEM((PEER_SUB * nsel, d), I32),
            pltpu.VMEM((PEER_SUB * nsel, d), I32),
            pltpu.SemaphoreType.DMA((2,)),
        ],
        compiler_params=_params("arbitrary"),
        name="peer_experts",
    )(idx_flat, idx_flat, gate_t, h2, x2, gate, table3)


def _sc_gather_rows(table, idx_flat, row_start, n_rows):
    d = table.shape[1]
    n_workers = SC_CORES * SC_SUBCORES
    assert n_rows % (n_workers * SC_WINDOW) == 0
    per_worker = n_rows // n_workers
    mesh = plsc.VectorSubcoreMesh(core_axis_name="c", subcore_axis_name="s")

    @functools.partial(
        pl.kernel, mesh=mesh,
        out_type=jax.ShapeDtypeStruct((n_rows, d), table.dtype),
        scratch_types=[
            pltpu.VMEM((SC_WINDOW,), I32),
            pltpu.VMEM((SC_WINDOW, d), table.dtype),
            pltpu.SemaphoreType.DMA,
        ],
        name="sc_gather_rows",
    )
    def gather(table_hbm, idx_hbm, out_hbm, idx_v, rows_v, sem):
        base = (lax.axis_index("s") * SC_CORES + lax.axis_index("c")) * per_worker

        @pl.loop(0, per_worker // SC_WINDOW)
        def _(w):
            off = base + w * SC_WINDOW
            pltpu.sync_copy(idx_hbm.at[pl.ds(row_start + off, SC_WINDOW)], idx_v)
            pltpu.async_copy(table_hbm.at[idx_v], rows_v, sem).wait()
            pltpu.sync_copy(rows_v, out_hbm.at[pl.ds(off, SC_WINDOW)])

    return gather(table, idx_flat)


def _peer_staged_kernel(gate_ref, h_ref, x_ref, mod_ref, rows_hbm, o_ref, ring, sem):
    part = pl.program_id(1)
    parts = pl.num_programs(1)
    step = pl.program_id(0) * parts + part
    n_steps = pl.num_programs(0) * parts
    step_rows = PEER_STAGE_TOK * PEER_HEADS * PEER_TOPK

    def copy(s):
        slot = lax.rem(s, PEER_STAGE_SLOTS)
        src = rows_hbm.at[pl.ds(pl.multiple_of(s * step_rows, step_rows), step_rows)]
        return pltpu.make_async_copy(src, ring.at[slot], sem.at[slot])

    @pl.when(step == 0)
    def _():
        copy(step).start()
        copy(step + 1).start()

    @pl.when(step + 2 < n_steps)
    def _():
        copy(step + 2).start()

    copy(step).wait()
    rows = ring.at[lax.rem(step, PEER_STAGE_SLOTS)]
    for j in range(PEER_STAGE_TOK):
        _reduce_token(part * PEER_STAGE_TOK + j, j, rows, gate_ref, h_ref, x_ref, mod_ref, o_ref)


def _peer_experts_staged(rows, gate_t, h2, x2, gate, seq, tok_start, n_tok):
    d = x2.shape[1]
    nsel = PEER_HEADS * PEER_TOPK
    first = tok_start // PEER_TOK
    parts = PEER_TOK // PEER_STAGE_TOK
    return pl.pallas_call(
        _peer_staged_kernel,
        grid=(n_tok // PEER_TOK, parts),
        in_specs=[
            pl.BlockSpec((nsel, PEER_TOK), lambda i, p: (0, i)),
            pl.BlockSpec((PEER_TOK, d), lambda i, p: (i, 0)),
            pl.BlockSpec((PEER_TOK, d), lambda i, p: (first + i, 0)),
            pl.BlockSpec((1, 1, d), lambda i, p: (((first + i) * PEER_TOK) // seq, 0, 0)),
            pl.BlockSpec(memory_space=pl.ANY),
        ],
        out_specs=pl.BlockSpec((PEER_TOK, d), lambda i, p: (i, 0)),
        out_shape=jax.ShapeDtypeStruct((n_tok, d), F32),
        scratch_shapes=[
            pltpu.VMEM((PEER_STAGE_SLOTS, PEER_STAGE_TOK * nsel, d), I32),
            pltpu.SemaphoreType.DMA((PEER_STAGE_SLOTS,)),
        ],
        compiler_params=_params("arbitrary", "arbitrary"),
        name="peer_experts_staged",
    )(gate_t, h2, x2, gate, rows)


def _final_norm_kernel(x_ref, g_ref, o_ref):
    x = x_ref[...]
    ms = jnp.mean(x * x, axis=-1, keepdims=True)
    o_ref[...] = x * lax.rsqrt(ms + NORM_EPS) * g_ref[...]


def _final_norm(x2, g, tm=512):
    t, d = x2.shape
    return pl.pallas_call(
        _final_norm_kernel,
        grid=(t // tm,),
        in_specs=[pl.BlockSpec((tm, d), lambda i: (i, 0)), pl.BlockSpec((1, d), lambda i: (0, 0))],
        out_specs=pl.BlockSpec((tm, d), lambda i: (i, 0)),
        out_shape=jax.ShapeDtypeStruct((t, d), F32),
        compiler_params=_params("arbitrary"),
        name="final_norm",
    )(x2, g.reshape(1, d))


def _pack_expert_rows(u, v):
    hi = lax.bitcast_convert_type(u.astype(BF16), jnp.uint16).astype(jnp.uint32)
    lo = lax.bitcast_convert_type(v.astype(BF16), jnp.uint16).astype(jnp.uint32)
    return lax.bitcast_convert_type((hi << 16) | lo, I32)


def _reorder_in_cols(w):
    n_qkv = SWA_WIDTH + 2 * SWA_KV_WIDTH + 3 * SB_WIDTH
    return jnp.concatenate([w[:, n_qkv:], w[:, :n_qkv]], axis=1)


def kernel(x, c, positions, ada_w, ada_b, norm1_g, w_in, swa_sinks, w_branch_swa, w_branch_sb,
           w_out, norm2_g, peer_wq, peer_subkeys, peer_u, peer_v, final_g):
    b, s, d = x.shape
    depth = ada_w.shape[0]
    nsel = PEER_HEADS * PEER_TOPK
    mod = _ada_mod(c, ada_w, ada_b)
    pos3 = positions.reshape(b, s, 1)
    inv_freq = jnp.power(ROPE_THETA, -jnp.arange(HEAD_DIM // 2, dtype=F32) * (2.0 / HEAD_DIM))
    invf = jnp.tile(inv_freq, LANES // (HEAD_DIM // 2)).reshape(1, LANES)

    n_groups = PEER_GROUPS if b % PEER_GROUPS == 0 else 1
    bg = b // n_groups
    tg = bg * s
    xs = [x[g * bg:(g + 1) * bg].reshape(tg, d) for g in range(n_groups)]
    n_units = tg // PEER_ROUTE_TOK
    tc_units = max(1, int(n_units * PEER_TC_SHARE))
    bounds = [(tc_units + ((n_units - tc_units) * k) // PEER_SC_CHUNKS) * PEER_ROUTE_TOK
              for k in range(PEER_SC_CHUNKS + 1)]

    layers = []
    for l in range(depth):
        table = _pack_expert_rows(peer_u[l], peer_v[l])
        layers.append(dict(
            w_in=_reorder_in_cols(w_in[l]).astype(BF16), wa=w_branch_swa[l].astype(BF16),
            wb=w_branch_sb[l].astype(BF16), wo=w_out[l].astype(BF16), wq=peer_wq[l].astype(BF16),
            sk=peer_subkeys[l].reshape(PEER_HEADS * 2, PEER_NKEYS, PEER_HALF).astype(BF16),
            table=table, table3=table[:, None, :]))

    def modulation(l, g):
        m = mod[l, g * bg:(g + 1) * bg].reshape(bg, N_MOD, 1, d)
        return [m[:, i] for i in range(N_MOD)]

    def attention(l, g):
        w = layers[l]
        shift1, scale1, gate1 = modulation(l, g)[:3]
        proj = _norm_proj(xs[g], norm1_g[l], scale1, shift1, w["w_in"], s)
        proj3 = proj.reshape(bg, s, IN_COLS)
        y_a = _swa(proj3, pos3[g * bg:(g + 1) * bg], invf, swa_sinks[l])
        y_b = _stick_breaking(proj3)
        xs[g] = _merge(y_a.reshape(tg, SWA_WIDTH), y_b.reshape(tg, SB_WIDTH), proj, xs[g], gate1,
                       w["wa"], w["wb"], w["wo"], s)

    def peer_start(l, g):
        w = layers[l]
        _, _, _, shift2, scale2, _ = modulation(l, g)
        route = lambda start, n_tok: _peer_route(xs[g], norm2_g[l], scale2, shift2, w["wq"], w["sk"], s,
                                                 start, n_tok)
        sc_ranges = []
        for lo, hi in zip(bounds[:-1], bounds[1:]):
            h_r, idx_r, gate_r = route(lo, hi - lo)
            staged = _sc_gather_rows(w["table"], idx_r.T.reshape(-1), 0, (hi - lo) * nsel)
            sc_ranges.append((staged, gate_r, h_r, lo, hi - lo))
        return sc_ranges, route(0, bounds[0])

    def peer_finish(l, g, state):
        sc_ranges, (h_tc, idx_tc, gate_tc) = state
        gate2 = modulation(l, g)[5]
        parts = [_peer_experts(idx_tc.T.reshape(-1), gate_tc, h_tc, xs[g], gate2, layers[l]["table3"], s,
                               bounds[0])]
        for staged, gate_r, h_r, start, n_tok in sc_ranges:
            parts.append(_peer_experts_staged(staged, gate_r, h_r, xs[g], gate2, s, start, n_tok))
        xs[g] = jnp.concatenate(parts, axis=0)

    states = {}
    for l in range(depth):
        for g in range(n_groups):
            if l > 0:
                peer_finish(l - 1, g, states.pop((l - 1, g)))
            attention(l, g)
            states[(l, g)] = peer_start(l, g)
    for g in range(n_groups):
        peer_finish(depth - 1, g, states.pop((depth - 1, g)))
    return _final_norm(jnp.concatenate(xs, axis=0), final_g).reshape(b, s, d)
```
